```python
import jax, jax.numpy as jnp
from jax import lax
import numpy as np

D_MODEL = 4096
BATCH = 4
SEQ = 2048
DEPTH = 4
DEC_BATCH = 128
DEC_SEQ = 4
PAST_LEN = 16384
PAGE_SIZE = 128

GDN_HEADS = 16
GDN_HEAD_DIM = 128
GDN_WIDTH = GDN_HEADS * GDN_HEAD_DIM
GDN_CONV_DIM = 3 * GDN_WIDTH
CONV_WIDTH = 4
GLA_HEADS = 4
GLA_DK = D_MODEL // 2 // GLA_HEADS
GLA_DV = D_MODEL // GLA_HEADS
GLA_KW = GLA_HEADS * GLA_DK
GLA_VW = GLA_HEADS * GLA_DV
GLA_GATE_RANK = 16
GLA_GATE_TAU = 16.0
CHUNK = 64
N_GROUPS = 4
EXPERTS_PER_GROUP = 8
N_EXPERTS = N_GROUPS * EXPERTS_PER_GROUP
TOP_K_IN_GROUP = 2
D_EXPERT = 768
DEEPNORM_ALPHA = (2 * DEPTH) ** 0.25
DEEPNORM_BETA = (8 * DEPTH) ** -0.25
LN_EPS = 1e-5
RMS_EPS = 1e-6
IN_SPLITS = (GDN_CONV_DIM, GDN_HEADS, GDN_HEADS, GDN_WIDTH, GLA_KW, GLA_KW, GLA_VW, GLA_GATE_RANK, GLA_VW, D_MODEL, D_MODEL)
IN_COLS = sum(IN_SPLITS)

kernel_name = 'hybrid_gdn_gla_hmoe_decode_step'


def layer_norm(x, g, b):
    xf = x.astype(jnp.float32)
    mu = jnp.mean(xf, axis=-1, keepdims=True)
    var = jnp.mean(jnp.square(xf - mu), axis=-1, keepdims=True)
    return ((xf - mu) * lax.rsqrt(var + LN_EPS) * g + b).astype(x.dtype)


def rms_norm(x, g):
    xf = x.astype(jnp.float32)
    return (xf * lax.rsqrt(jnp.mean(xf * xf, axis=-1, keepdims=True) + RMS_EPS) * g).astype(x.dtype)


def l2_normalize(x):
    xf = x.astype(jnp.float32)
    return (xf * lax.rsqrt(jnp.sum(xf * xf, axis=-1, keepdims=True) + RMS_EPS)).astype(x.dtype)


def split_columns(h):
    outs, start = [], 0
    for size in IN_SPLITS:
        outs.append(h[..., start:start + size])
        start += size
    return outs


def causal_short_conv(x, buf, w):
    L = x.shape[1]
    full = jnp.concatenate([buf.astype(x.dtype), x], axis=1)
    out = full[:, 0:L] * w[0]
    for j in range(1, CONV_WIDTH):
        out = out + full[:, j:j + L] * w[j]
    return jax.nn.silu(out), full[:, L:]


def to_chunks(x, c):
    b, L = x.shape[:2]
    n = -(-L // c)
    x = jnp.pad(x, [(0, 0), (0, n * c - L)] + [(0, 0)] * (x.ndim - 2))
    return jnp.moveaxis(x.reshape((b, n, c) + x.shape[2:]), 1, 0)


def from_chunks(y, L):
    n, b, c = y.shape[:3]
    return jnp.moveaxis(y, 0, 1).reshape((b, n * c) + y.shape[3:])[:, :L]


def gated_delta_chunked(q, k, v, beta, log_a, s0):
    L = q.shape[1]
    c = min(CHUNK, L)
    xs = tuple(to_chunks(t.astype(jnp.float32), c) for t in (q, k, v, beta, log_a))
    causal = jnp.tril(jnp.ones((c, c), dtype=bool))
    strict = jnp.tril(jnp.ones((c, c), dtype=bool), k=-1)
    eye = jnp.eye(c, dtype=jnp.float32)

    def step(S, inp):
        qc, kc, vc, bc, ac = inp
        g = jnp.moveaxis(jnp.cumsum(ac, axis=1), 1, 2)
        bt = jnp.moveaxis(bc, 1, 2)
        qh = jnp.moveaxis(qc, 1, 2)
        kh = jnp.moveaxis(kc, 1, 2)
        vh = jnp.moveaxis(vc, 1, 2)
        decay = jnp.exp(jnp.where(causal, g[..., :, None] - g[..., None, :], -jnp.inf))
        a_mat = jnp.where(strict, bt[..., :, None] * jnp.einsum('bhtk,bhsk->bhts', kh, kh) * decay, 0.0)
        rhs = jnp.concatenate([bt[..., None] * vh, (bt * jnp.exp(g))[..., None] * kh], axis=-1)
        sol = lax.linalg.triangular_solve(eye + a_mat, rhs, left_side=True, lower=True)
        u, w = sol[..., :GDN_HEAD_DIM], sol[..., GDN_HEAD_DIM:]
        v_new = u - jnp.einsum('bhtk,bhkv->bhtv', w, S)
        o = (jnp.einsum('bhtk,bhkv->bhtv', qh * jnp.exp(g)[..., None], S)
             + jnp.einsum('bhts,bhsv->bhtv', jnp.einsum('bhtk,bhsk->bhts', qh, kh) * decay, v_new))
        g_last = g[..., -1:]
        S = S * jnp.exp(g_last)[..., None] + jnp.einsum('bhtk,bhtv->bhkv', kh * jnp.exp(g_last - g)[..., None], v_new)
        return S, jnp.moveaxis(o, 1, 2)

    s_fin, o = lax.scan(step, s0.astype(jnp.float32), xs)
    return from_chunks(o, L), s_fin.astype(s0.dtype)


def gla_chunked(q, k, v, log_f, s0):
    L = q.shape[1]
    c = min(CHUNK, L)
    xs = tuple(to_chunks(t.astype(jnp.float32), c) for t in (q, k, v, log_f))
    causal = jnp.tril(jnp.ones((c, c), dtype=bool))[:, :, None, None]

    def step(S, inp):
        qc, kc, vc, fc = inp
        b = jnp.cumsum(fc, axis=1)
        inter = jnp.einsum('bthk,bhkv->bthv', qc * jnp.exp(b), S)
        rel = jnp.exp(jnp.where(causal, b[:, :, None] - b[:, None, :], -jnp.inf))
        scores = jnp.einsum('bthk,bshk,btshk->bhts', qc, kc, rel)
        intra = jnp.einsum('bhts,bshv->bthv', scores, vc)
        b_last = b[:, -1]
        S = S * jnp.exp(b_last)[..., None] + jnp.einsum('bthk,bthv->bhkv', kc * jnp.exp(b_last[:, None] - b), vc)
        return S, inter + intra

    s_fin, o = lax.scan(step, s0.astype(jnp.float32), xs)
    return from_chunks(o, L), s_fin.astype(s0.dtype)


def token_mixer(x, conv_buf, s_gdn, s_gla, w_in, gdn_conv_w, gdn_a_log, gdn_dt_bias, gdn_norm_g,
                gla_gate_w2, gla_gate_b2, gla_norm_g, w_proj_gdn, w_proj_gla, w_out):
    B, L, _ = x.shape
    dt = x.dtype
    h = x @ w_in
    qkv, beta_in, a_in, z, gq, gk, gv, glr, gr, merge_a, merge_b = split_columns(h)
    qkv, new_conv = causal_short_conv(qkv, conv_buf, gdn_conv_w)
    q = l2_normalize(qkv[..., :GDN_WIDTH].reshape(B, L, GDN_HEADS, GDN_HEAD_DIM)) * (GDN_HEAD_DIM ** -0.5)
    k = l2_normalize(qkv[..., GDN_WIDTH:2 * GDN_WIDTH].reshape(B, L, GDN_HEADS, GDN_HEAD_DIM))
    v = qkv[..., 2 * GDN_WIDTH:].reshape(B, L, GDN_HEADS, GDN_HEAD_DIM)
    beta = jax.nn.sigmoid(beta_in.astype(jnp.float32))
    log_a = -jnp.exp(gdn_a_log.astype(jnp.float32)) * jax.nn.softplus(a_in.astype(jnp.float32) + gdn_dt_bias)
    o_gdn, s_gdn_new = gated_delta_chunked(q, k, v, beta, log_a, s_gdn)
    o_gdn = rms_norm(o_gdn.astype(dt), gdn_norm_g) * jax.nn.silu(z.reshape(B, L, GDN_HEADS, GDN_HEAD_DIM))
    log_f = jax.nn.log_sigmoid((glr @ gla_gate_w2 + gla_gate_b2).astype(jnp.float32)) / GLA_GATE_TAU
    o_gla, s_gla_new = gla_chunked(gq.reshape(B, L, GLA_HEADS, GLA_DK) * (GLA_DK ** -0.5),
                                   gk.reshape(B, L, GLA_HEADS, GLA_DK),
                                   gv.reshape(B, L, GLA_HEADS, GLA_DV),
                                   log_f.reshape(B, L, GLA_HEADS, GLA_DK), s_gla)
    o_gla = rms_norm(o_gla.astype(dt), gla_norm_g) * jax.nn.silu(gr.reshape(B, L, GLA_HEADS, GLA_DV))
    y_a = o_gdn.reshape(B, L, GDN_WIDTH) @ w_proj_gdn
    y_b = o_gla.reshape(B, L, GLA_VW) @ w_proj_gla
    merged = jax.nn.sigmoid(merge_a) * y_a + jax.nn.sigmoid(merge_b) * y_b
    return merged @ w_out, new_conv, s_gdn_new, s_gla_new


def hierarchical_moe(x, router_group_w, router_group_b, router_expert_w, router_expert_b,
                     expert_w_gate, expert_w_up, expert_w_down):
    B, L, D = x.shape
    xt = x.reshape(B * L, D)
    n = xt.shape[0]
    group_logits = (xt @ router_group_w).astype(jnp.float32) + router_group_b
    group_p = jax.nn.softmax(group_logits, axis=-1)
    g_sel = jnp.argmax(group_logits, axis=-1)
    p_sel = jnp.take_along_axis(group_p, g_sel[:, None], axis=1)
    exp_logits = ((xt @ router_expert_w).astype(jnp.float32) + router_expert_b).reshape(n, N_GROUPS, EXPERTS_PER_GROUP)
    in_group = jnp.take_along_axis(exp_logits, g_sel[:, None, None], axis=1)[:, 0]
    top_v, top_i = lax.top_k(in_group, TOP_K_IN_GROUP)
    weights = jax.nn.softmax(top_v, axis=-1) * p_sel
    expert_id = g_sel[:, None] * EXPERTS_PER_GROUP + top_i
    combine = jnp.einsum('nk,nke->ne', weights, jax.nn.one_hot(expert_id, N_EXPERTS, dtype=jnp.float32)).astype(x.dtype)
    out = jnp.zeros_like(xt)
    for e in range(N_EXPERTS):
        hid = jax.nn.silu(xt @ expert_w_gate[e]) * (xt @ expert_w_up[e])
        out = out + combine[:, e:e + 1] * (hid @ expert_w_down[e])
    return out.reshape(B, L, D)


def trunk_layer(x, conv_buf, s_gdn, s_gla, w_in, gdn_conv_w, gdn_a_log, gdn_dt_bias, gdn_norm_g,
                gla_gate_w2, gla_gate_b2, gla_norm_g, w_proj_gdn, w_proj_gla, w_out, ln_mix_g, ln_mix_b,
                router_group_w, router_group_b, router_expert_w, router_expert_b,
                expert_w_gate, expert_w_up, expert_w_down, ln_ffn_g, ln_ffn_b):
    mix, new_conv, new_gdn, new_gla = token_mixer(x, conv_buf, s_gdn, s_gla, w_in, gdn_conv_w, gdn_a_log, gdn_dt_bias,
                                                  gdn_norm_g, gla_gate_w2, gla_gate_b2, gla_norm_g,
                                                  w_proj_gdn, w_proj_gla, w_out)
    x = layer_norm(DEEPNORM_ALPHA * x + mix, ln_mix_g, ln_mix_b)
    ffn = hierarchical_moe(x, router_group_w, router_group_b, router_expert_w, router_expert_b,
                           expert_w_gate, expert_w_up, expert_w_down)
    x = layer_norm(DEEPNORM_ALPHA * x + ffn, ln_ffn_g, ln_ffn_b)
    return x, new_conv, new_gdn, new_gla


def setup_inputs(seed: int = 0) -> dict:
    key = jax.random.key(seed)
    ks = jax.random.split(key, 28)
    f32 = jnp.float32

    def nrm(k, shape, scale):
        return jax.random.normal(k, shape, f32) * scale

    beta = DEEPNORM_BETA
    col_scale = jnp.concatenate([
        jnp.ones((2 * GDN_WIDTH,), f32), jnp.full((GDN_WIDTH,), beta, f32),
        jnp.ones((2 * GDN_HEADS + GDN_WIDTH + 2 * GLA_KW,), f32), jnp.full((GLA_VW,), beta, f32),
        jnp.ones((GLA_GATE_RANK + GLA_VW + 2 * D_MODEL,), f32)])
    dt0 = jnp.exp(jax.random.uniform(ks[8], (DEPTH, GDN_HEADS), f32, np.log(1e-3), np.log(1e-1)))
    return {
        'x_prompt': nrm(ks[0], (BATCH, SEQ, D_MODEL), 1.0),
        'x_sample': nrm(ks[1], (DEC_BATCH, DEC_SEQ, D_MODEL), 1.0),
        'state_gdn_conv': nrm(ks[2], (DEPTH, DEC_BATCH, CONV_WIDTH - 1, GDN_CONV_DIM), 1.0),
        'state_gdn': nrm(ks[3], (DEPTH, DEC_BATCH, GDN_HEADS, GDN_HEAD_DIM, GDN_HEAD_DIM), 0.1),
        'state_gla': nrm(ks[4], (DEPTH, DEC_BATCH, GLA_HEADS, GLA_DK, GLA_DV), 0.1),
        'w_in': nrm(ks[5], (DEPTH, D_MODEL, IN_COLS), D_MODEL ** -0.5) * col_scale,
        'gdn_conv_w': nrm(ks[6], (DEPTH, CONV_WIDTH, GDN_CONV_DIM), CONV_WIDTH ** -0.5),
        'gdn_a_log': jnp.log(jax.random.uniform(ks[7], (DEPTH, GDN_HEADS), f32, 1.0, 16.0)),
        'gdn_dt_bias': dt0 + jnp.log(-jnp.expm1(-dt0)),
        'gdn_norm_g': 1.0 + nrm(ks[9], (DEPTH, GDN_HEAD_DIM), 0.1),
        'gla_gate_w2': nrm(ks[10], (DEPTH, GLA_GATE_RANK, GLA_KW), GLA_GATE_RANK ** -0.5),
        'gla_gate_b2': nrm(ks[11], (DEPTH, GLA_KW), 0.1),
        'gla_norm_g': 1.0 + nrm(ks[12], (DEPTH, GLA_DV), 0.1),
        'w_proj_gdn': nrm(ks[13], (DEPTH, GDN_WIDTH, D_MODEL), beta * GDN_WIDTH ** -0.5),
        'w_proj_gla': nrm(ks[14], (DEPTH, GLA_VW, D_MODEL), beta * GLA_VW ** -0.5),
        'w_out': nrm(ks[15], (DEPTH, D_MODEL, D_MODEL), beta * D_MODEL ** -0.5),
        'ln_mix_g': 1.0 + nrm(ks[16], (DEPTH, D_MODEL), 0.1),
        'ln_mix_b': nrm(ks[17], (DEPTH, D_MODEL), 0.01),
        'router_group_w': nrm(ks[18], (DEPTH, D_MODEL, N_GROUPS), D_MODEL ** -0.5),
        'router_group_b': nrm(ks[19], (DEPTH, N_GROUPS), 0.01),
        'router_expert_w': nrm(ks[20], (DEPTH, D_MODEL, N_EXPERTS), D_MODEL ** -0.5),
        'router_expert_b': nrm(ks[21], (DEPTH, N_EXPERTS), 0.01),
        'expert_w_gate': nrm(ks[22], (DEPTH, N_EXPERTS, D_MODEL, D_EXPERT), D_MODEL ** -0.5),
        'expert_w_up': nrm(ks[23], (DEPTH, N_EXPERTS, D_MODEL, D_EXPERT), beta * D_MODEL ** -0.5),
        'expert_w_down': nrm(ks[24], (DEPTH, N_EXPERTS, D_EXPERT, D_MODEL), beta * D_EXPERT ** -0.5),
        'ln_ffn_g': 1.0 + nrm(ks[25], (DEPTH, D_MODEL), 0.1),
        'ln_ffn_b': nrm(ks[26], (DEPTH, D_MODEL), 0.01),
    }


def reference(x_prompt, x_sample, state_gdn_conv, state_gdn, state_gla, w_in, gdn_conv_w, gdn_a_log, gdn_dt_bias,
              gdn_norm_g, gla_gate_w2, gla_gate_b2, gla_norm_g, w_proj_gdn, w_proj_gla, w_out, ln_mix_g, ln_mix_b,
              router_group_w, router_group_b, router_expert_w, router_expert_b,
              expert_w_gate, expert_w_up, expert_w_down, ln_ffn_g, ln_ffn_b):
    b_p = x_prompt.shape[0]
    dt = x_prompt.dtype
    xp, xs = x_prompt, x_sample
    conv_p, gdn_p, gla_p, conv_s, gdn_s, gla_s = [], [], [], [], [], []
    for l in range(DEPTH):
        lp = (w_in[l], gdn_conv_w[l], gdn_a_log[l], gdn_dt_bias[l], gdn_norm_g[l], gla_gate_w2[l], gla_gate_b2[l],
              gla_norm_g[l], w_proj_gdn[l], w_proj_gla[l], w_out[l], ln_mix_g[l], ln_mix_b[l],
              router_group_w[l], router_group_b[l], router_expert_w[l], router_expert_b[l],
              expert_w_gate[l], expert_w_up[l], expert_w_down[l], ln_ffn_g[l], ln_ffn_b[l])
        c0 = jnp.zeros((b_p, CONV_WIDTH - 1, GDN_CONV_DIM), dt)
        g0 = jnp.zeros((b_p, GDN_HEADS, GDN_HEAD_DIM, GDN_HEAD_DIM), dt)
        s0 = jnp.zeros((b_p, GLA_HEADS, GLA_DK, GLA_DV), dt)
        xp, cp, gp, sp = trunk_layer(xp, c0, g0, s0, *lp)
        xs, cs, gs, ss = trunk_layer(xs, state_gdn_conv[l], state_gdn[l], state_gla[l], *lp)
        conv_p.append(cp); gdn_p.append(gp); gla_p.append(sp)
        conv_s.append(cs); gdn_s.append(gs); gla_s.append(ss)
    return (xp, xs, jnp.stack(conv_p), jnp.stack(gdn_p), jnp.stack(gla_p),
            jnp.stack(conv_s), jnp.stack(gdn_s), jnp.stack(gla_s))
```

```python
import functools
import math

import jax
import jax.numpy as jnp
from jax import lax
from jax.experimental import pallas as pl
from jax.experimental.pallas import tpu as pltpu

F32 = jnp.float32
BF16 = jnp.bfloat16
HI = lax.Precision.HIGHEST

V7X_VMEM_LIMIT_BYTES = 58 * 1024 * 1024
LANES = 128
SUBLANES = 8

CHUNK = 64
GLA_SUB = 16
LN_EPS = 1e-5
RMS_EPS = 1e-6
GLA_GATE_TAU = 16.0
MOE_ROW_TILE = 512

NT = (((1,), (1,)), ((), ()))
TN = (((0,), (0,)), ((), ()))


def _params(*sem):
    return pltpu.CompilerParams(dimension_semantics=sem, vmem_limit_bytes=V7X_VMEM_LIMIT_BYTES)


def _tile(n, target, align):
    t = (min(target, n) // align) * align
    while t >= align:
        if n % t == 0:
            return t
        t -= align
    return n


def _mm_kernel(a_ref, b_ref, o_ref):
    o_ref[...] = jnp.dot(a_ref[...], b_ref[...].astype(BF16),
                         preferred_element_type=F32).astype(o_ref.dtype)


def _matmul(a, b, *, layer=None, tm, tn, out_dtype=F32, name):
    m, k = a.shape
    n = b.shape[-1]
    if layer is None:
        b_spec = pl.BlockSpec((k, tn), lambda i, j: (0, j))
    else:
        b_spec = pl.BlockSpec((None, k, tn), lambda i, j: (layer, 0, j))
    return pl.pallas_call(
        _mm_kernel,
        grid=(m // tm, n // tn),
        in_specs=[pl.BlockSpec((tm, k), lambda i, j: (i, 0)), b_spec],
        out_specs=pl.BlockSpec((tm, tn), lambda i, j: (i, j)),
        out_shape=jax.ShapeDtypeStruct((m, n), out_dtype),
        compiler_params=_params("parallel", "parallel"),
        name=name,
    )(a, b)


def _mm_hi_kernel(a_ref, b_ref, o_ref):
    o_ref[...] = jnp.dot(a_ref[...], b_ref[...], precision=HI, preferred_element_type=F32)


def _matmul_f32(a, b, *, tm, name):
    m, k = a.shape
    n = b.shape[-1]
    return pl.pallas_call(
        _mm_hi_kernel,
        grid=(m // tm,),
        in_specs=[pl.BlockSpec((tm, k), lambda i: (i, 0)), pl.BlockSpec((k, n), lambda i: (0, 0))],
        out_specs=pl.BlockSpec((tm, n), lambda i: (i, 0)),
        out_shape=jax.ShapeDtypeStruct((m, n), F32),
        compiler_params=_params("parallel"),
        name=name,
    )(a, b)


def _merge_kernel(og_ref, ol_ref, wg_ref, wl_ref, ma_ref, mb_ref, o_ref):
    ya = jnp.dot(og_ref[...], wg_ref[...].astype(BF16), preferred_element_type=F32)
    yb = jnp.dot(ol_ref[...], wl_ref[...].astype(BF16), preferred_element_type=F32)
    o_ref[...] = (jax.nn.sigmoid(ma_ref[...]) * ya + jax.nn.sigmoid(mb_ref[...]) * yb).astype(o_ref.dtype)


def _merge_proj(og, ol, w_pg, w_pl, h, ma_off, mb_off, *, layer, tm, tn):
    m, kg = og.shape
    kl = ol.shape[1]
    d = w_pg.shape[-1]
    ja, jb = ma_off // tn, mb_off // tn
    return pl.pallas_call(
        _merge_kernel,
        grid=(m // tm, d // tn),
        in_specs=[
            pl.BlockSpec((tm, kg), lambda i, j: (i, 0)),
            pl.BlockSpec((tm, kl), lambda i, j: (i, 0)),
            pl.BlockSpec((None, kg, tn), lambda i, j: (layer, 0, j)),
            pl.BlockSpec((None, kl, tn), lambda i, j: (layer, 0, j)),
            pl.BlockSpec((tm, tn), lambda i, j: (i, ja + j)),
            pl.BlockSpec((tm, tn), lambda i, j: (i, jb + j)),
        ],
        out_specs=pl.BlockSpec((tm, tn), lambda i, j: (i, j)),
        out_shape=jax.ShapeDtypeStruct((m, d), BF16),
        compiler_params=_params("parallel", "parallel"),
        name="merge_proj",
    )(og, ol, w_pg, w_pl, h, h)


def _ln_kernel(x_ref, r_ref, g_ref, b_ref, o_ref, *, alpha):
    y = alpha * x_ref[...] + r_ref[...]
    mu = jnp.mean(y, axis=-1, keepdims=True)
    yc = y - mu
    var = jnp.mean(yc * yc, axis=-1, keepdims=True)
    o_ref[...] = yc * lax.rsqrt(var + LN_EPS) * g_ref[...] + b_ref[...]


def _deepnorm_ln(x, r, g, b, *, alpha, tr):
    n, d = x.shape
    row = pl.BlockSpec((tr, d), lambda i: (i, 0))
    vec = pl.BlockSpec((1, d), lambda i: (0, 0))
    return pl.pallas_call(
        functools.partial(_ln_kernel, alpha=alpha),
        grid=(n // tr,),
        in_specs=[row, row, vec, vec],
        out_specs=row,
        out_shape=jax.ShapeDtypeStruct((n, d), F32),
        compiler_params=_params("parallel"),
        name="deepnorm_ln",
    )(x, r, g.reshape(1, d), b.reshape(1, d))


def _gdn_kernel(q_ref, k_ref, v_ref, g_ref, b_ref, s0_ref, o_ref, so_ref, s_acc, *, c, hb, dg):
    ci = pl.program_id(2)

    @pl.when(ci == 0)
    def _():
        s_acc[...] = s0_ref[...]

    rows = LANES
    per_pack = rows // c
    lc = int(math.log2(c))
    r = lax.broadcasted_iota(jnp.int32, (rows, rows), 0)
    s = lax.broadcasted_iota(jnp.int32, (rows, rows), 1)
    head_r = lax.shift_right_logical(r, lc)
    same = head_r == lax.shift_right_logical(s, lc)
    eye = r == s
    causal = jnp.logical_and(same, s <= r)
    strict = jnp.logical_and(same, s < r)
    last = s == (lax.shift_left(head_r, lc) + (c - 1))
    ident = jnp.where(eye, 1.0, 0.0).astype(F32)

    def to_col(mask, row):
        return jnp.sum(jnp.where(mask, row, 0.0), axis=1, keepdims=True)

    for p in range(hb // per_pack):
        heads = [p * per_pack + j for j in range(per_pack)]
        grow = g_ref[p:p + 1, :]
        brow = b_ref[p:p + 1, :]
        gcol = to_col(eye, grow)
        bcol = to_col(eye, brow)
        glast = to_col(last, grow)
        stack = lambda ref: jnp.concatenate([ref[:, h * dg:(h + 1) * dg] for h in heads], axis=0)
        qp, kp, vp = stack(q_ref), stack(k_ref), stack(v_ref)

        decay = jnp.exp(jnp.where(causal, gcol - grow, -1e30))
        kk = lax.dot_general(kp, kp, NT, precision=HI, preferred_element_type=F32)
        a = jnp.where(strict, bcol * kk * decay, 0.0)
        x = ident - a
        pw = jnp.dot(a, a, precision=HI, preferred_element_type=F32)
        for it in range(lc - 1):
            x = x + jnp.dot(x, pw, precision=HI, preferred_element_type=F32)
            if it < lc - 2:
                pw = jnp.dot(pw, pw, precision=HI, preferred_element_type=F32)
        eg = jnp.exp(gcol)
        rhs = jnp.concatenate([bcol * vp, (bcol * eg) * kp], axis=1)
        sol = jnp.dot(x, rhs, precision=HI, preferred_element_type=F32)
        u, w = sol[:, :dg], sol[:, dg:]
        qk = lax.dot_general(qp.astype(BF16), kp.astype(BF16), NT, preferred_element_type=F32)
        pm = jnp.where(causal, qk * decay, 0.0)
        qg = qp * eg

        vnew, qs = [], []
        for j, h in enumerate(heads):
            sl = slice(j * c, (j + 1) * c)
            lhs = jnp.concatenate([w[sl], qg[sl]], axis=0).astype(BF16)
            res = jnp.dot(lhs, s_acc[h].astype(BF16), preferred_element_type=F32)
            vnew.append(u[sl] - res[:c])
            qs.append(res[c:])
        vnew = jnp.concatenate(vnew, axis=0)
        out = jnp.concatenate(qs, axis=0) + jnp.dot(pm.astype(BF16), vnew.astype(BF16),
                                                    preferred_element_type=F32)
        kd = kp * jnp.exp(glast - gcol)
        for j, h in enumerate(heads):
            sl = slice(j * c, (j + 1) * c)
            ds = lax.dot_general(kd[sl].astype(BF16), vnew[sl].astype(BF16), TN, preferred_element_type=F32)
            s_acc[h] = s_acc[h] * jnp.exp(glast[j * c:j * c + 1, :]) + ds
            o_ref[:, h * dg:(h + 1) * dg] = out[sl]

    @pl.when(ci == pl.num_programs(2) - 1)
    def _():
        so_ref[...] = s_acc[...]


def _gdn_scan(q, k, v, grow, brow, s0, *, c, hb):
    bsz, lp, width = q.shape
    nheads, dg = s0.shape[1], s0.shape[2]
    nc = lp // c
    packs = hb * c // LANES
    seq = pl.BlockSpec((None, c, hb * dg), lambda b, h, i: (b, i, h))
    vec = pl.BlockSpec((None, None, None, packs, LANES), lambda b, h, i: (b, i, h, 0, 0))
    st = pl.BlockSpec((None, hb, dg, dg), lambda b, h, i: (b, h, 0, 0))
    return pl.pallas_call(
        functools.partial(_gdn_kernel, c=c, hb=hb, dg=dg),
        grid=(bsz, nheads // hb, nc),
        in_specs=[seq, seq, seq, vec, vec, st],
        out_specs=[seq, st],
        out_shape=[jax.ShapeDtypeStruct((bsz, lp, width), F32),
                   jax.ShapeDtypeStruct(s0.shape, F32)],
        scratch_shapes=[pltpu.VMEM((hb, dg, dg), F32)],
        compiler_params=_params("parallel", "parallel", "arbitrary"),
        name="gdn_scan",
    )(q, k, v, grow, brow, s0)


def _gla_kernel(q_ref, k_ref, v_ref, f_ref, s0_ref, o_ref, so_ref, s_acc, *, c, hb, dk, dv, sb):
    ci = pl.program_id(2)

    @pl.when(ci == 0)
    def _():
        s_acc[...] = s0_ref[...]

    r = lax.broadcasted_iota(jnp.int32, (c, c), 0)
    s = lax.broadcasted_iota(jnp.int32, (c, c), 1)
    tri = jnp.where(s <= r, 1.0, 0.0).astype(F32)

    for j in range(hb):
        q = q_ref[:, j * dk:(j + 1) * dk]
        k = k_ref[:, j * dk:(j + 1) * dk]
        v = v_ref[:, j * dv:(j + 1) * dv]
        b = jnp.dot(tri, f_ref[:, j * dk:(j + 1) * dk], precision=HI, preferred_element_type=F32)
        sj = s_acc[j]
        inter = jnp.dot((q * jnp.exp(b)).astype(BF16), sj.astype(BF16), preferred_element_type=F32)
        blocks = []
        for i in range(c // sb):
            lo, hi = i * sb, (i + 1) * sb
            b0 = b[lo:lo + 1, :]
            qi = (q[lo:hi] * jnp.exp(b[lo:hi] - b0)).astype(BF16)
            kc = (k[:hi] * jnp.exp(b0 - b[:hi])).astype(BF16)
            sc = lax.dot_general(qi, kc, NT, preferred_element_type=F32)
            t_idx = lax.broadcasted_iota(jnp.int32, (sb, hi), 0) + lo
            s_idx = lax.broadcasted_iota(jnp.int32, (sb, hi), 1)
            sc = jnp.where(s_idx <= t_idx, sc, 0.0)
            blocks.append(jnp.dot(sc.astype(BF16), v[:hi].astype(BF16), preferred_element_type=F32))
        o_ref[:, j * dv:(j + 1) * dv] = inter + jnp.concatenate(blocks, axis=0)
        bt = b.T
        bl = bt[:, c - 1:c]
        kdt = (k.T * jnp.exp(bl - bt)).astype(BF16)
        s_acc[j] = sj * jnp.exp(bl) + jnp.dot(kdt, v.astype(BF16), preferred_element_type=F32)

    @pl.when(ci == pl.num_programs(2) - 1)
    def _():
        so_ref[...] = s_acc[...]


def _gla_scan(q, k, v, lf, s0, *, c, hb):
    bsz, lp, _ = q.shape
    nheads, dk, dv = s0.shape[1], s0.shape[2], s0.shape[3]
    kseq = pl.BlockSpec((None, c, hb * dk), lambda b, h, i: (b, i, h))
    vseq = pl.BlockSpec((None, c, hb * dv), lambda b, h, i: (b, i, h))
    st = pl.BlockSpec((None, hb, dk, dv), lambda b, h, i: (b, h, 0, 0))
    return pl.pallas_call(
        functools.partial(_gla_kernel, c=c, hb=hb, dk=dk, dv=dv, sb=min(GLA_SUB, c)),
        grid=(bsz, nheads // hb, lp // c),
        in_specs=[kseq, kseq, vseq, kseq, st],
        out_specs=[vseq, st],
        out_shape=[jax.ShapeDtypeStruct((bsz, lp, nheads * dv), F32),
                   jax.ShapeDtypeStruct(s0.shape, F32)],
        scratch_shapes=[pltpu.VMEM((hb, dk, dv), F32)],
        compiler_params=_params("parallel", "parallel", "arbitrary"),
        name="gla_scan",
    )(q, k, v, lf, s0)


def _moe_kernel(te_ref, nt_ref, x_ref, wg_ref, wu_ref, wd_ref, o_ref):
    i, f = pl.program_id(0), pl.program_id(1)

    @pl.when(f == 0)
    def _():
        o_ref[...] = jnp.zeros_like(o_ref)

    @pl.when(i < nt_ref[0])
    def _():
        x = x_ref[...]
        g = jnp.dot(x, wg_ref[...].astype(BF16), preferred_element_type=F32)
        u = jnp.dot(x, wu_ref[...].astype(BF16), preferred_element_type=F32)
        hid = (jax.nn.silu(g) * u).astype(BF16)
        o_ref[...] += jnp.dot(hid, wd_ref[...].astype(BF16), preferred_element_type=F32)


def _moe_experts(xs, tile_expert, n_tiles, w_gate, w_up, w_down, *, layer, tm, tf):
    t, d = xs.shape
    fdim = w_gate.shape[-1]
    grid_spec = pltpu.PrefetchScalarGridSpec(
        num_scalar_prefetch=2,
        grid=(t // tm, fdim // tf),
        in_specs=[
            pl.BlockSpec((tm, d), lambda i, f, te, nt: (i, 0)),
            pl.BlockSpec((None, None, d, tf), lambda i, f, te, nt: (layer, te[i], 0, f)),
            pl.BlockSpec((None, None, d, tf), lambda i, f, te, nt: (layer, te[i], 0, f)),
            pl.BlockSpec((None, None, tf, d), lambda i, f, te, nt: (layer, te[i], f, 0)),
        ],
        out_specs=pl.BlockSpec((tm, d), lambda i, f, te, nt: (i, 0)),
    )
    return pl.pallas_call(
        _moe_kernel,
        grid_spec=grid_spec,
        out_shape=jax.ShapeDtypeStruct((t, d), F32),
        compiler_params=_params("arbitrary", "arbitrary"),
        name="moe_experts",
    )(tile_expert, n_tiles, xs, w_gate, w_up, w_down)


def _chunk_len(length):
    c = SUBLANES
    while c < min(length, CHUNK):
        c *= 2
    return c


def _pad_time(x, lp):
    pad = lp - x.shape[1]
    if pad == 0:
        return x
    return jnp.pad(x, [(0, 0), (0, pad)] + [(0, 0)] * (x.ndim - 2))


def _silu(x):
    return x * jax.nn.sigmoid(x)


def _l2n(x):
    return x * lax.rsqrt(jnp.sum(x * x, axis=-1, keepdims=True) + RMS_EPS)


def _rms(x, g):
    return x * lax.rsqrt(jnp.mean(x * x, axis=-1, keepdims=True) + RMS_EPS) * g


def _mixer_group(hb_, hs_, conv_buf, s_gdn, s_gla, lw, dims):
    (hg, dg, hl, dk, dv, rank) = dims
    bsz, length, _ = hb_.shape
    gw, kw, vw = hg * dg, hl * dk, hl * dv
    o = 0
    qkv_pre = hb_[..., o:o + 3 * gw]; o += 3 * gw
    z = hb_[..., o:o + gw]; o += gw
    gq = hb_[..., o:o + kw]; o += kw
    gk = hb_[..., o:o + kw]; o += kw
    gv = hb_[..., o:o + vw]; o += vw
    gr = hb_[..., o:o + vw]; o += vw
    beta_in, a_in, glr = hs_[..., :hg], hs_[..., hg:2 * hg], hs_[..., 2 * hg:2 * hg + rank]

    conv_w = lw["gdn_conv_w"]
    width = conv_w.shape[0]
    full = jnp.concatenate([conv_buf, qkv_pre], axis=1)
    acc = full[:, 0:length] * conv_w[0]
    for j in range(1, width):
        acc = acc + full[:, j:j + length] * conv_w[j]
    qkv = _silu(acc)
    new_conv = full[:, length:]

    c = _chunk_len(length)
    lp = -(-length // c) * c
    nc = lp // c
    q = _l2n(qkv[..., :gw].reshape(bsz, length, hg, dg)) * (dg ** -0.5)
    k = _l2n(qkv[..., gw:2 * gw].reshape(bsz, length, hg, dg))
    v = qkv[..., 2 * gw:]
    beta = jax.nn.sigmoid(beta_in)
    log_a = -jnp.exp(lw["gdn_a_log"]) * jax.nn.softplus(a_in + lw["gdn_dt_bias"])
    q = _pad_time(q.reshape(bsz, length, gw), lp)
    k = _pad_time(k.reshape(bsz, length, gw), lp)
    v = _pad_time(v, lp)
    beta = _pad_time(beta, lp)
    log_a = _pad_time(log_a, lp)
    hb = hg if c < CHUNK else min(hg, 4)
    hb = max(hb, LANES // c)
    g = jnp.cumsum(log_a.reshape(bsz, nc, c, hg), axis=2)

    def lane_rows(t):
        return jnp.swapaxes(t, 2, 3).reshape(bsz, nc, hg // hb, hb * c // LANES, LANES)

    o_gdn, s_gdn_new = _gdn_scan(q, k, v, lane_rows(g), lane_rows(beta.reshape(bsz, nc, c, hg)),
                                 s_gdn, c=c, hb=hb)
    o_gdn = o_gdn[:, :length].reshape(bsz, length, hg, dg)
    o_gdn = _rms(o_gdn, lw["gdn_norm_g"]) * _silu(z.reshape(bsz, length, hg, dg))

    lf = jax.nn.log_sigmoid(jnp.dot(glr, lw["gla_gate_w2"], precision=HI) + lw["gla_gate_b2"]) / GLA_GATE_TAU
    o_gla, s_gla_new = _gla_scan(_pad_time(gq * (dk ** -0.5), lp), _pad_time(gk, lp), _pad_time(gv, lp),
                                 _pad_time(lf, lp), s_gla, c=c, hb=1 if c == CHUNK else min(hl, 2))
    o_gla = o_gla[:, :length].reshape(bsz, length, hl, dv)
    o_gla = _rms(o_gla, lw["gla_norm_g"]) * _silu(gr.reshape(bsz, length, hl, dv))
    return (o_gdn.reshape(bsz * length, gw).astype(BF16), o_gla.reshape(bsz * length, vw).astype(BF16),
            new_conv, s_gdn_new, s_gla_new)


def _moe(x1, lw, layer, weights):
    n, d = x1.shape
    rg_w, re_w = lw["router_group_w"], lw["router_expert_w"]
    n_groups, n_exp = rg_w.shape[-1], re_w.shape[-1]
    per_group = n_exp // n_groups
    rw = jnp.concatenate([rg_w, re_w], axis=1)
    rw = jnp.pad(rw, ((0, 0), (0, LANES - rw.shape[1] % LANES)))
    logits = _matmul_f32(x1, rw, tm=_tile(n, 512, SUBLANES), name="router")
    group_logits = logits[:, :n_groups] + lw["router_group_b"]
    group_p = jax.nn.softmax(group_logits, axis=-1)
    g_sel = jnp.argmax(group_logits, axis=-1)
    p_sel = jnp.take_along_axis(group_p, g_sel[:, None], axis=1)
    exp_logits = (logits[:, n_groups:n_groups + n_exp] + lw["router_expert_b"]).reshape(n, n_groups, per_group)
    in_group = jnp.take_along_axis(exp_logits, g_sel[:, None, None], axis=1)[:, 0]
    top_v, top_i = lax.top_k(in_group, 2)
    wts = jax.nn.softmax(top_v, axis=-1) * p_sel
    expert_id = (g_sel[:, None] * per_group + top_i).astype(jnp.int32)

    tm = MOE_ROW_TILE
    flat_e = expert_id.reshape(-1)
    na = flat_e.shape[0]
    counts = jnp.zeros((n_exp,), jnp.int32).at[flat_e].add(1)
    tiles_per = (counts + tm - 1) // tm
    tile_end = jnp.cumsum(tiles_per)
    row_start = (tile_end - tiles_per) * tm
    order = jnp.argsort(flat_e, stable=True)
    sorted_e = flat_e[order]
    first = jnp.cumsum(counts) - counts
    rank = jnp.arange(na, dtype=jnp.int32) - first[sorted_e]
    dest_sorted = row_start[sorted_e] + rank
    n_rows_tiles = (na + n_exp * (tm - 1)) // tm + 1
    t_rows = n_rows_tiles * tm
    row_token = jnp.zeros((t_rows,), jnp.int32).at[dest_sorted].set(order // 2)
    dest = jnp.zeros((na,), jnp.int32).at[order].set(dest_sorted)
    n_tiles = tile_end[-1:]
    tile_ids = jnp.arange(n_rows_tiles, dtype=jnp.int32)
    tile_expert = jnp.searchsorted(tile_end, jnp.minimum(tile_ids, n_tiles[0] - 1), side="right").astype(jnp.int32)
    tile_expert = jnp.minimum(tile_expert, n_exp - 1)

    xs = x1.astype(BF16)[row_token]
    fdim = weights["expert_w_gate"].shape[-1]
    ys = _moe_experts(xs, tile_expert, n_tiles.astype(jnp.int32), weights["expert_w_gate"],
                      weights["expert_w_up"], weights["expert_w_down"], layer=layer, tm=tm,
                      tf=_tile(fdim, 256, LANES))
    picked = ys[dest].reshape(n, 2, d)
    return picked[:, 0] * wts[:, 0:1] + picked[:, 1] * wts[:, 1:2]


def kernel(x_prompt, x_sample, state_gdn_conv, state_gdn, state_gla, w_in, gdn_conv_w, gdn_a_log, gdn_dt_bias,
           gdn_norm_g, gla_gate_w2, gla_gate_b2, gla_norm_g, w_proj_gdn, w_proj_gla, w_out, ln_mix_g, ln_mix_b,
           router_group_w, router_group_b, router_expert_w, router_expert_b,
           expert_w_gate, expert_w_up, expert_w_down, ln_ffn_g, ln_ffn_b):
    depth, d, _ = w_in.shape
    bp, lp_, _ = x_prompt.shape
    bs, ls_, _ = x_sample.shape
    hg, dg = gdn_a_log.shape[-1], gdn_norm_g.shape[-1]
    gw = hg * dg
    rank, kw = gla_gate_w2.shape[1], gla_gate_w2.shape[2]
    dv = gla_norm_g.shape[-1]
    vw = w_proj_gla.shape[1]
    hl = vw // dv
    dk = kw // hl
    width = gdn_conv_w.shape[1]
    alpha = (2 * depth) ** 0.25
    dims = (hg, dg, hl, dk, dv, rank)

    np_, ns_ = bp * lp_, bs * ls_
    n = np_ + ns_
    x = jnp.concatenate([x_prompt.reshape(np_, d), x_sample.reshape(ns_, d)], axis=0)

    sizes = (3 * gw, hg, hg, gw, kw, kw, vw, rank, vw, d, d)
    offs = [0]
    for sz in sizes:
        offs.append(offs[-1] + sz)
    big_order = (0, 3, 4, 5, 6, 8, 9, 10)
    small_order = (1, 2, 7)
    big_cols = sum(sizes[i] for i in big_order)
    ma_off = sum(sizes[i] for i in big_order[:-2])
    mb_off = ma_off + d
    small_cols = sum(sizes[i] for i in small_order)

    tm = _tile(n, 1088, SUBLANES)
    zeros_conv = jnp.zeros((bp, width - 1, 3 * gw), F32)
    zeros_gdn = jnp.zeros((bp, hg, dg, dg), F32)
    zeros_gla = jnp.zeros((bp, hl, dk, dv), F32)

    outs = [[] for _ in range(6)]
    for l in range(depth):
        lw = dict(gdn_conv_w=gdn_conv_w[l], gdn_a_log=gdn_a_log[l], gdn_dt_bias=gdn_dt_bias[l],
                  gdn_norm_g=gdn_norm_g[l], gla_gate_w2=gla_gate_w2[l], gla_gate_b2=gla_gate_b2[l],
                  gla_norm_g=gla_norm_g[l], router_group_w=router_group_w[l], router_group_b=router_group_b[l],
                  router_expert_w=router_expert_w[l], router_expert_b=router_expert_b[l])
        wl = w_in[l]
        w_big = jnp.concatenate([wl[:, offs[i]:offs[i + 1]] for i in big_order], axis=1).astype(BF16)
        w_small = jnp.concatenate([wl[:, offs[i]:offs[i + 1]] for i in small_order], axis=1)
        w_small = jnp.pad(w_small, ((0, 0), (0, LANES - small_cols))).astype(BF16)

        xb = x.astype(BF16)
        h_big = _matmul(xb, w_big, tm=tm, tn=_tile(big_cols, 512, LANES), name="in_proj")
        h_small = _matmul(xb, w_small, tm=tm, tn=LANES, name="in_proj_small")

        og_p, ol_p, cp, gp, sp = _mixer_group(h_big[:np_].reshape(bp, lp_, big_cols),
                                              h_small[:np_].reshape(bp, lp_, LANES),
                                              zeros_conv, zeros_gdn, zeros_gla, lw, dims)
        og_s, ol_s, cs, gs, ss = _mixer_group(h_big[np_:].reshape(bs, ls_, big_cols),
                                              h_small[np_:].reshape(bs, ls_, LANES),
                                              state_gdn_conv[l], state_gdn[l], state_gla[l], lw, dims)
        og = jnp.concatenate([og_p, og_s], axis=0)
        ol = jnp.concatenate([ol_p, ol_s], axis=0)
        tn_d = _tile(d, 256, LANES)
        merged = _merge_proj(og, ol, w_proj_gdn, w_proj_gla, h_big, ma_off, mb_off, layer=l, tm=tm, tn=tn_d)
        mix = _matmul(merged, w_out, layer=l, tm=tm, tn=_tile(d, 512, LANES), name="out_proj")
        tr = _tile(n, 256, SUBLANES)
        x1 = _deepnorm_ln(x, mix, ln_mix_g[l], ln_mix_b[l], alpha=alpha, tr=tr)
        ffn = _moe(x1, lw, l, dict(expert_w_gate=expert_w_gate, expert_w_up=expert_w_up,
                                   expert_w_down=expert_w_down))
        x = _deepnorm_ln(x1, ffn, ln_ffn_g[l], ln_ffn_b[l], alpha=alpha, tr=tr)
        for lst, val in zip(outs, (cp, gp, sp, cs, gs, ss)):
            lst.append(val)

    return (x[:np_].reshape(bp, lp_, d), x[np_:].reshape(bs, ls_, d)) + tuple(jnp.stack(o) for o in outs)
```

```python
import functools
import math

import jax
import jax.numpy as jnp
from jax import lax
from jax.experimental import pallas as pl
from jax.experimental.pallas import tpu as pltpu

F32 = jnp.float32
BF16 = jnp.bfloat16
HI = lax.Precision.HIGHEST

V7X_VMEM_LIMIT_BYTES = 58 * 1024 * 1024
LANES = 128
SUBLANES = 8

CHUNK = 64
GLA_CHUNK = 128
GLA_SUB = 16
LN_EPS = 1e-5
RMS_EPS = 1e-6
GLA_GATE_TAU = 16.0
MOE_ROW_TILE = 512

NT = (((1,), (1,)), ((), ()))
TN = (((0,), (0,)), ((), ()))


def _params(*sem):
    return pltpu.CompilerParams(dimension_semantics=sem, vmem_limit_bytes=V7X_VMEM_LIMIT_BYTES)


def _tile(n, target, align):
    t = (min(target, n) // align) * align
    while t >= align:
        if n % t == 0:
            return t
        t -= align
    return n


def _silu(x):
    return x * jax.nn.sigmoid(x)


def _l2n(x):
    return x * lax.rsqrt(jnp.sum(x * x, axis=-1, keepdims=True) + RMS_EPS)


def _rms(x, g):
    return x * lax.rsqrt(jnp.mean(x * x, axis=-1, keepdims=True) + RMS_EPS) * g


def _mm_kernel(a_ref, b_ref, o_ref):
    o_ref[...] = jnp.dot(a_ref[...], b_ref[...].astype(BF16),
                         preferred_element_type=F32).astype(o_ref.dtype)


def _matmul(a, b, *, layer=None, tm, tn, out_dtype=F32, name):
    m, k = a.shape
    n = b.shape[-1]
    if layer is None:
        b_spec = pl.BlockSpec((k, tn), lambda i, j: (0, j))
    else:
        b_spec = pl.BlockSpec((None, k, tn), lambda i, j: (layer, 0, j))
    return pl.pallas_call(
        _mm_kernel,
        grid=(m // tm, n // tn),
        in_specs=[pl.BlockSpec((tm, k), lambda i, j: (i, 0)), b_spec],
        out_specs=pl.BlockSpec((tm, tn), lambda i, j: (i, j)),
        out_shape=jax.ShapeDtypeStruct((m, n), out_dtype),
        compiler_params=_params("parallel", "parallel"),
        name=name,
    )(a, b)


def _mm_hi_kernel(a_ref, b_ref, o_ref):
    o_ref[...] = jnp.dot(a_ref[...], b_ref[...], precision=HI, preferred_element_type=F32)


def _matmul_f32(a, b, *, tm, name):
    m, k = a.shape
    n = b.shape[-1]
    return pl.pallas_call(
        _mm_hi_kernel,
        grid=(m // tm,),
        in_specs=[pl.BlockSpec((tm, k), lambda i: (i, 0)), pl.BlockSpec((k, n), lambda i: (0, 0))],
        out_specs=pl.BlockSpec((tm, n), lambda i: (i, 0)),
        out_shape=jax.ShapeDtypeStruct((m, n), F32),
        compiler_params=_params("parallel"),
        name=name,
    )(a, b)


def _merge_kernel(og_ref, ol_ref, wg_ref, wl_ref, ma_ref, mb_ref, o_ref):
    ya = jnp.dot(og_ref[...], wg_ref[...].astype(BF16), preferred_element_type=F32)
    yb = jnp.dot(ol_ref[...], wl_ref[...].astype(BF16), preferred_element_type=F32)
    o_ref[...] = (jax.nn.sigmoid(ma_ref[...]) * ya + jax.nn.sigmoid(mb_ref[...]) * yb).astype(o_ref.dtype)


def _merge_proj(og, ol, w_pg, w_pl, h, ma_off, mb_off, *, layer, tm, tn):
    m, kg = og.shape
    kl = ol.shape[1]
    d = w_pg.shape[-1]
    ja, jb = ma_off // tn, mb_off // tn
    return pl.pallas_call(
        _merge_kernel,
        grid=(m // tm, d // tn),
        in_specs=[
            pl.BlockSpec((tm, kg), lambda i, j: (i, 0)),
            pl.BlockSpec((tm, kl), lambda i, j: (i, 0)),
            pl.BlockSpec((None, kg, tn), lambda i, j: (layer, 0, j)),
            pl.BlockSpec((None, kl, tn), lambda i, j: (layer, 0, j)),
            pl.BlockSpec((tm, tn), lambda i, j: (i, ja + j)),
            pl.BlockSpec((tm, tn), lambda i, j: (i, jb + j)),
        ],
        out_specs=pl.BlockSpec((tm, tn), lambda i, j: (i, j)),
        out_shape=jax.ShapeDtypeStruct((m, d), BF16),
        compiler_params=_params("parallel", "parallel"),
        name="merge_proj",
    )(og, ol, w_pg, w_pl, h, h)


def _ln_kernel(x_ref, r_ref, g_ref, b_ref, o_ref, ob_ref, *, alpha):
    y = alpha * x_ref[...] + r_ref[...].astype(F32)
    mu = jnp.mean(y, axis=-1, keepdims=True)
    yc = y - mu
    var = jnp.mean(yc * yc, axis=-1, keepdims=True)
    out = yc * lax.rsqrt(var + LN_EPS) * g_ref[...] + b_ref[...]
    o_ref[...] = out
    ob_ref[...] = out.astype(BF16)


def _deepnorm_ln(x, r, g, b, *, alpha, tr):
    n, d = x.shape
    row = pl.BlockSpec((tr, d), lambda i: (i, 0))
    vec = pl.BlockSpec((1, d), lambda i: (0, 0))
    return pl.pallas_call(
        functools.partial(_ln_kernel, alpha=alpha),
        grid=(n // tr,),
        in_specs=[row, row, vec, vec],
        out_specs=[row, row],
        out_shape=[jax.ShapeDtypeStruct((n, d), F32), jax.ShapeDtypeStruct((n, d), BF16)],
        compiler_params=_params("parallel"),
        name="deepnorm_ln",
    )(x, r, g.reshape(1, d), b.reshape(1, d))


def _split2(a):
    hi = a.astype(BF16)
    return hi, (a - hi.astype(F32)).astype(BF16)


def _dot3(a, b, dims=None):
    def d(x, y):
        if dims is None:
            return jnp.dot(x, y, preferred_element_type=F32)
        return lax.dot_general(x, y, dims, preferred_element_type=F32)
    return d(a[0], b[0]) + (d(a[0], b[1]) + d(a[1], b[0]))


def _cumsum_rows(tri, x):
    hi = x.astype(BF16)
    r1 = x - hi.astype(F32)
    mid = r1.astype(BF16)
    lo = (r1 - mid.astype(F32)).astype(BF16)
    d = lambda y: jnp.dot(tri, y, preferred_element_type=F32)
    return d(hi) + (d(mid) + d(lo))


def _tri(c):
    r = lax.broadcasted_iota(jnp.int32, (c, c), 0)
    s = lax.broadcasted_iota(jnp.int32, (c, c), 1)
    return jnp.where(s <= r, 1.0, 0.0).astype(BF16)


def _pack_masks(c):
    lc = int(math.log2(c))
    r = lax.broadcasted_iota(jnp.int32, (LANES, LANES), 0)
    s = lax.broadcasted_iota(jnp.int32, (LANES, LANES), 1)
    same = lax.shift_right_logical(r, lc) == lax.shift_right_logical(s, lc)
    eye = r == s
    levels = []
    for lvl in range(1, lc + 1):
        rr = lax.shift_right_logical(r, lvl - 1)
        ss = lax.shift_right_logical(s, lvl - 1)
        levels.append(jnp.logical_and(rr == ss + 1, jnp.bitwise_and(ss, 1) == 0))
    return dict(eye=eye, causal=jnp.logical_and(same, s <= r), strict=jnp.logical_and(same, s < r),
                last=s == (lax.shift_left(lax.shift_right_logical(r, lc), lc) + (c - 1)),
                ident=jnp.where(eye, 1.0, 0.0).astype(F32), levels=levels)


def _unit_lower_inverse(a_list, mk):
    t = [mk["ident"] - jnp.where(mk["levels"][0], a, 0.0) for a in a_list]
    for m in mk["levels"][1:]:
        ts = [_split2(x) for x in t]
        tl = [_dot3(x, _split2(jnp.where(m, a, 0.0))) for x, a in zip(ts, a_list)]
        t = [x - _dot3(_split2(y), xs) for x, y, xs in zip(t, tl, ts)]
    return t


def _gdn_packs(packs, c, dg, mk):
    causal, strict = mk["causal"], mk["strict"]
    bdot = lambda x, y: jnp.dot(x.astype(BF16), y.astype(BF16), preferred_element_type=F32)
    decay = [jnp.exp(jnp.where(causal, p["gcol"] - p["grow"], -1e30)) for p in packs]
    ks = [_split2(p["k"]) for p in packs]
    kk = [_dot3(x, x, NT) for x in ks]
    a = [jnp.where(strict, p["bcol"] * y * d, 0.0) for p, y, d in zip(packs, kk, decay)]
    t = _unit_lower_inverse(a, mk)
    eg = [jnp.exp(p["gcol"]) for p in packs]
    rhs = [jnp.concatenate([p["bcol"] * p["v"], (p["bcol"] * e) * p["k"]], axis=1) for p, e in zip(packs, eg)]
    sol = [_dot3(_split2(x), _split2(y)) for x, y in zip(t, rhs)]
    qk = [lax.dot_general(p["q"].astype(BF16), x[0], NT, preferred_element_type=F32) for p, x in zip(packs, ks)]
    pm = [jnp.where(causal, x * d, 0.0) for x, d in zip(qk, decay)]
    qg = [p["q"] * e for p, e in zip(packs, eg)]
    rows = [slice(j * c, (j + 1) * c) for j in range(LANES // c)]
    res = [[bdot(jnp.concatenate([s[:, dg:][sl], g[sl]], axis=0), p["state"][h])
            for sl, h in zip(rows, p["heads"])] for p, s, g in zip(packs, sol, qg)]
    vnew = [jnp.concatenate([s[:, :dg][sl] - r[:c] for sl, r in zip(rows, rs)], axis=0)
            for s, rs in zip(sol, res)]
    out = [jnp.concatenate([r[c:] for r in rs], axis=0) + bdot(m, vn) for rs, m, vn in zip(res, pm, vnew)]
    kd = [p["k"] * jnp.exp(p["glast"] - p["gcol"]) for p in packs]
    ds = [[lax.dot_general(x[sl].astype(BF16), vn[sl].astype(BF16), TN, preferred_element_type=F32)
           for sl in rows] for x, vn in zip(kd, vnew)]
    for p, dsp in zip(packs, ds):
        for sl, h, d in zip(rows, p["heads"], dsp):
            p["state"][h] = p["state"][h] * jnp.exp(p["glast"][sl][0:1, :]) + d
    return out


def _gdn_kernel(q_ref, k_ref, v_ref, g_ref, b_ref, s0_ref, o_ref, so_ref, s_acc, *, c, bb, hb, dg):
    ci = pl.program_id(2)

    @pl.when(ci == 0)
    def _():
        s_acc[...] = s0_ref[...]

    per_pack = LANES // c
    mk = _pack_masks(c)

    def to_col(mask, row):
        return jnp.sum(jnp.where(mask, row, 0.0), axis=1, keepdims=True)

    packs = []
    for bi in range(bb):
        for p in range(hb // per_pack):
            heads = [p * per_pack + j for j in range(per_pack)]
            grow = g_ref[bi, p:p + 1, :]
            brow = b_ref[bi, p:p + 1, :]
            stack = lambda ref: jnp.concatenate([ref[bi, :, h * dg:(h + 1) * dg] for h in heads], axis=0)
            packs.append(dict(q=stack(q_ref), k=stack(k_ref), v=stack(v_ref), gcol=to_col(mk["eye"], grow),
                              grow=grow, bcol=to_col(mk["eye"], brow), glast=to_col(mk["last"], grow),
                              state=s_acc.at[bi], heads=heads, bi=bi))
    for p, out in zip(packs, _gdn_packs(packs, c, dg, mk)):
        for j, h in enumerate(p["heads"]):
            o_ref[p["bi"], :, h * dg:(h + 1) * dg] = out[j * c:(j + 1) * c]

    @pl.when(ci == pl.num_programs(2) - 1)
    def _():
        so_ref[...] = s_acc[...]


def _gdn_scan(q, k, v, grow, brow, s0, *, c, bb, hb):
    bsz, lp, width = q.shape
    nheads, dg = s0.shape[1], s0.shape[2]
    nc = lp // c
    packs = hb * c // LANES
    seq = pl.BlockSpec((bb, c, hb * dg), lambda b, h, i: (b, i, h))
    vec = pl.BlockSpec((bb, None, None, packs, LANES), lambda b, h, i: (b, i, h, 0, 0))
    st = pl.BlockSpec((bb, hb, dg, dg), lambda b, h, i: (b, h, 0, 0))
    return pl.pallas_call(
        functools.partial(_gdn_kernel, c=c, bb=bb, hb=hb, dg=dg),
        grid=(bsz // bb, nheads // hb, nc),
        in_specs=[seq, seq, seq, vec, vec, st],
        out_specs=[seq, st],
        out_shape=[jax.ShapeDtypeStruct((bsz, lp, width), F32),
                   jax.ShapeDtypeStruct(s0.shape, F32)],
        scratch_shapes=[pltpu.VMEM((bb, hb, dg, dg), F32)],
        compiler_params=_params("parallel", "parallel", "arbitrary"),
        name="gdn_scan",
    )(q, k, v, grow, brow, s0)


def _gdn_prompt_kernel(hq_ref, hk_ref, hv_ref, hz_ref, hs_ref, wq_ref, wk_ref, wv_ref, alog_ref, dtb_ref,
                       ng_ref, o_ref, so_ref, cq_ref, ck_ref, cv_ref, s_acc, pq, pk, pv, *, c, hg, dg, width):
    ci = pl.program_id(1)

    @pl.when(ci == 0)
    def _():
        s_acc[...] = jnp.zeros_like(s_acc)
        pq[...] = jnp.zeros_like(pq)
        pk[...] = jnp.zeros_like(pk)
        pv[...] = jnp.zeros_like(pv)

    per_pack = LANES // c
    mk = _pack_masks(c)
    hs = hs_ref[...]
    beta_all = jax.nn.sigmoid(hs)
    la_all = -jnp.exp(alog_ref[...]) * jax.nn.softplus(hs + dtb_ref[...])
    g_all = _cumsum_rows(_tri(c), la_all)

    def conv(h_ref, prev_ref, w_ref, cs):
        cur = h_ref[:, cs]
        ext = jnp.concatenate([prev_ref[:, cs], cur], axis=0)
        w = w_ref[:, cs]
        acc = cur * w[width - 1:width]
        for j in range(1, width):
            acc = acc + ext[SUBLANES - j:SUBLANES - j + c] * w[width - 1 - j:width - j]
        return _silu(acc)

    packs = []
    for p in range(hg // per_pack):
        heads = [p * per_pack + j for j in range(per_pack)]
        cols = [slice(h * dg, (h + 1) * dg) for h in heads]
        gcol = jnp.concatenate([g_all[:, hg + h:hg + h + 1] for h in heads], axis=0)
        packs.append(dict(
            q=jnp.concatenate([_l2n(conv(hq_ref, pq, wq_ref, cs)) * (dg ** -0.5) for cs in cols], axis=0),
            k=jnp.concatenate([_l2n(conv(hk_ref, pk, wk_ref, cs)) for cs in cols], axis=0),
            v=jnp.concatenate([conv(hv_ref, pv, wv_ref, cs) for cs in cols], axis=0),
            gcol=gcol, grow=jnp.sum(jnp.where(mk["eye"], gcol, 0.0), axis=0, keepdims=True),
            bcol=jnp.concatenate([beta_all[:, h:h + 1] for h in heads], axis=0),
            glast=jnp.concatenate([jnp.broadcast_to(g_all[c - 1:c, hg + h:hg + h + 1], (c, 1)) for h in heads],
                                  axis=0),
            state=s_acc, heads=heads, cols=cols))
    for p, out in zip(packs, _gdn_packs(packs, c, dg, mk)):
        for j, cs in enumerate(p["cols"]):
            o = out[j * c:(j + 1) * c]
            o_ref[:, cs] = (_rms(o, ng_ref[...]) * _silu(hz_ref[:, cs])).astype(o_ref.dtype)

    tail = slice(c - SUBLANES, c)
    pq[...] = hq_ref[tail, :]
    pk[...] = hk_ref[tail, :]
    pv[...] = hv_ref[tail, :]

    @pl.when(ci == pl.num_programs(1) - 1)
    def _():
        so_ref[...] = s_acc[...]
        cq_ref[...] = hq_ref[tail, :]
        ck_ref[...] = hk_ref[tail, :]
        cv_ref[...] = hv_ref[tail, :]


def _gdn_prompt(h_big, h_small, conv_w, a_log, dt_bias, norm_g, *, bsz, length, hg, dg):
    gw = hg * dg
    c = CHUNK
    nc = length // c
    width = conv_w.shape[0]
    pad = lambda vrow: jnp.pad(vrow.reshape(1, hg), ((0, 0), (hg, LANES - 2 * hg)))
    col = lambda k: pl.BlockSpec((c, gw), lambda b, i: (b * nc + i, k))
    wcol = lambda k: pl.BlockSpec((width, gw), lambda b, i: (0, k))
    vec = pl.BlockSpec((1, LANES), lambda b, i: (0, 0))
    tail = pl.BlockSpec((None, SUBLANES, gw), lambda b, i: (b, 0, 0))
    tail_shape = jax.ShapeDtypeStruct((bsz, SUBLANES, gw), F32)
    return pl.pallas_call(
        functools.partial(_gdn_prompt_kernel, c=c, hg=hg, dg=dg, width=width),
        grid=(bsz, nc),
        in_specs=[col(0), col(1), col(2), col(3),
                  pl.BlockSpec((c, LANES), lambda b, i: (b * nc + i, 0)),
                  wcol(0), wcol(1), wcol(2), vec, vec,
                  pl.BlockSpec((1, dg), lambda b, i: (0, 0))],
        out_specs=[pl.BlockSpec((c, gw), lambda b, i: (b * nc + i, 0)),
                   pl.BlockSpec((None, hg, dg, dg), lambda b, i: (b, 0, 0, 0)),
                   tail, tail, tail],
        out_shape=[jax.ShapeDtypeStruct((bsz * length, gw), BF16),
                   jax.ShapeDtypeStruct((bsz, hg, dg, dg), F32),
                   tail_shape, tail_shape, tail_shape],
        scratch_shapes=[pltpu.VMEM((hg, dg, dg), F32)] + [pltpu.VMEM((SUBLANES, gw), F32)] * 3,
        compiler_params=_params("parallel", "arbitrary"),
        name="gdn_prompt",
    )(h_big, h_big, h_big, h_big, h_small, conv_w, conv_w, conv_w, pad(a_log), pad(dt_bias),
      norm_g.reshape(1, dg))


def _gla_chunk(q, k, v, lf, sj, tri, c, sb):
    b = _cumsum_rows(tri, lf)
    inter = jnp.dot((q * jnp.exp(b)).astype(BF16), sj.astype(BF16), preferred_element_type=F32)
    blocks = []
    for i in range(c // sb):
        lo, hi = i * sb, (i + 1) * sb
        b0 = b[lo:lo + 1, :]
        qi = (q[lo:hi] * jnp.exp(b[lo:hi] - b0)).astype(BF16)
        kc = (k[:hi] * jnp.exp(b0 - b[:hi])).astype(BF16)
        sc = lax.dot_general(qi, kc, NT, preferred_element_type=F32)
        t_idx = lax.broadcasted_iota(jnp.int32, (sb, hi), 0) + lo
        s_idx = lax.broadcasted_iota(jnp.int32, (sb, hi), 1)
        sc = jnp.where(s_idx <= t_idx, sc, 0.0)
        blocks.append(jnp.dot(sc.astype(BF16), v[:hi].astype(BF16), preferred_element_type=F32))
    bt = b.T
    bl = bt[:, c - 1:c]
    kdt = (k.T * jnp.exp(bl - bt)).astype(BF16)
    s_new = sj * jnp.exp(bl) + jnp.dot(kdt, v.astype(BF16), preferred_element_type=F32)
    return inter + jnp.concatenate(blocks, axis=0), s_new


def _gla_kernel(q_ref, k_ref, v_ref, f_ref, s0_ref, o_ref, so_ref, s_acc, *, c, hb, dk, dv, sb):
    ci = pl.program_id(2)

    @pl.when(ci == 0)
    def _():
        s_acc[...] = s0_ref[...]

    tri = _tri(c)
    for j in range(hb):
        ks, vs = slice(j * dk, (j + 1) * dk), slice(j * dv, (j + 1) * dv)
        out, s_new = _gla_chunk(q_ref[:, ks], k_ref[:, ks], v_ref[:, vs], f_ref[:, ks], s_acc[j], tri, c, sb)
        o_ref[:, vs] = out
        s_acc[j] = s_new

    @pl.when(ci == pl.num_programs(2) - 1)
    def _():
        so_ref[...] = s_acc[...]


def _gla_scan(q, k, v, lf, s0, *, c, hb):
    bsz, lp, _ = q.shape
    nheads, dk, dv = s0.shape[1], s0.shape[2], s0.shape[3]
    kseq = pl.BlockSpec((None, c, hb * dk), lambda b, h, i: (b, i, h))
    vseq = pl.BlockSpec((None, c, hb * dv), lambda b, h, i: (b, i, h))
    st = pl.BlockSpec((None, hb, dk, dv), lambda b, h, i: (b, h, 0, 0))
    return pl.pallas_call(
        functools.partial(_gla_kernel, c=c, hb=hb, dk=dk, dv=dv, sb=min(GLA_SUB, c)),
        grid=(bsz, nheads // hb, lp // c),
        in_specs=[kseq, kseq, vseq, kseq, st],
        out_specs=[vseq, st],
        out_shape=[jax.ShapeDtypeStruct((bsz, lp, nheads * dv), F32),
                   jax.ShapeDtypeStruct(s0.shape, F32)],
        scratch_shapes=[pltpu.VMEM((hb, dk, dv), F32)],
        compiler_params=_params("parallel", "parallel", "arbitrary"),
        name="gla_scan",
    )(q, k, v, lf, s0)


def _gla_prompt_kernel(hq_ref, hk_ref, hv_ref, hr_ref, hs_ref, w2_ref, b2_ref, ng_ref, o_ref, so_ref, s_acc,
                       *, c, dk, sb):
    ci = pl.program_id(2)

    @pl.when(ci == 0)
    def _():
        s_acc[...] = jnp.zeros_like(s_acc)

    gate = jnp.dot(hs_ref[...], w2_ref[...], precision=HI, preferred_element_type=F32) + b2_ref[...]
    lf = jax.nn.log_sigmoid(gate) * (1.0 / GLA_GATE_TAU)
    out, s_new = _gla_chunk(hq_ref[...] * (dk ** -0.5), hk_ref[...], hv_ref[...], lf, s_acc[...], _tri(c), c, sb)
    s_acc[...] = s_new
    o_ref[...] = (_rms(out, ng_ref[...]) * _silu(hr_ref[...])).astype(o_ref.dtype)

    @pl.when(ci == pl.num_programs(2) - 1)
    def _():
        so_ref[...] = s_new


def _gla_prompt(h_big, h_small, w2_rows, b2, norm_g, offs, *, bsz, length, hl, dk, dv):
    c = GLA_CHUNK
    nc = length // c
    oq, ok, ov, og = offs
    kcol = lambda off: pl.BlockSpec((c, dk), lambda b, h, i: (b * nc + i, off // dk + h))
    vcol = lambda off: pl.BlockSpec((c, dv), lambda b, h, i: (b * nc + i, off // dv + h))
    return pl.pallas_call(
        functools.partial(_gla_prompt_kernel, c=c, dk=dk, sb=GLA_SUB),
        grid=(bsz, hl, nc),
        in_specs=[kcol(oq), kcol(ok), vcol(ov), vcol(og),
                  pl.BlockSpec((c, LANES), lambda b, h, i: (b * nc + i, 0)),
                  pl.BlockSpec((LANES, dk), lambda b, h, i: (0, h)),
                  pl.BlockSpec((1, dk), lambda b, h, i: (0, h)),
                  pl.BlockSpec((1, dv), lambda b, h, i: (0, 0))],
        out_specs=[pl.BlockSpec((c, dv), lambda b, h, i: (b * nc + i, h)),
                   pl.BlockSpec((None, None, dk, dv), lambda b, h, i: (b, h, 0, 0))],
        out_shape=[jax.ShapeDtypeStruct((bsz * length, hl * dv), BF16),
                   jax.ShapeDtypeStruct((bsz, hl, dk, dv), F32)],
        scratch_shapes=[pltpu.VMEM((dk, dv), F32)],
        compiler_params=_params("parallel", "parallel", "arbitrary"),
        name="gla_prompt",
    )(h_big, h_big, h_big, h_big, h_small, w2_rows, b2.reshape(1, hl * dk), norm_g.reshape(1, dv))


def _moe_kernel(te_ref, nt_ref, x_ref, wg_ref, wu_ref, wd_ref, o_ref, acc):
    i, f = pl.program_id(0), pl.program_id(1)

    @pl.when(f == 0)
    def _():
        acc[...] = jnp.zeros_like(acc)

    @pl.when(i < nt_ref[0])
    def _():
        x = x_ref[...]
        g = jnp.dot(x, wg_ref[...].astype(BF16), preferred_element_type=F32)
        u = jnp.dot(x, wu_ref[...].astype(BF16), preferred_element_type=F32)
        hid = (_silu(g) * u).astype(BF16)
        acc[...] += jnp.dot(hid, wd_ref[...].astype(BF16), preferred_element_type=F32)

    @pl.when(f == pl.num_programs(1) - 1)
    def _():
        o_ref[...] = acc[...].astype(o_ref.dtype)


def _moe_experts(xs, tile_expert, n_tiles, w_gate, w_up, w_down, *, layer, tm, tf):
    t, d = xs.shape
    fdim = w_gate.shape[-1]
    grid_spec = pltpu.PrefetchScalarGridSpec(
        num_scalar_prefetch=2,
        grid=(t // tm, fdim // tf),
        in_specs=[
            pl.BlockSpec((tm, d), lambda i, f, te, nt: (i, 0)),
            pl.BlockSpec((None, None, d, tf), lambda i, f, te, nt: (layer, te[i], 0, f)),
            pl.BlockSpec((None, None, d, tf), lambda i, f, te, nt: (layer, te[i], 0, f)),
            pl.BlockSpec((None, None, tf, d), lambda i, f, te, nt: (layer, te[i], f, 0)),
        ],
        out_specs=pl.BlockSpec((tm, d), lambda i, f, te, nt: (i, 0)),
        scratch_shapes=[pltpu.VMEM((tm, d), F32)],
    )
    return pl.pallas_call(
        _moe_kernel,
        grid_spec=grid_spec,
        out_shape=jax.ShapeDtypeStruct((t, d), BF16),
        compiler_params=_params("arbitrary", "arbitrary"),
        name="moe_experts",
    )(tile_expert, n_tiles, xs, w_gate, w_up, w_down)


def _chunk_len(length):
    c = SUBLANES
    while c < min(length, CHUNK):
        c *= 2
    return c


def _pad_time(x, lp):
    pad = lp - x.shape[1]
    if pad == 0:
        return x
    return jnp.pad(x, [(0, 0), (0, pad)] + [(0, 0)] * (x.ndim - 2))


def _mixer_group(hb_, hs_, conv_buf, s_gdn, s_gla, lw, dims):
    (hg, dg, hl, dk, dv, rank) = dims
    bsz, length, _ = hb_.shape
    gw, kw, vw = hg * dg, hl * dk, hl * dv
    o = 0
    qkv_pre = hb_[..., o:o + 3 * gw]; o += 3 * gw
    z = hb_[..., o:o + gw]; o += gw
    gq = hb_[..., o:o + kw]; o += kw
    gk = hb_[..., o:o + kw]; o += kw
    gv = hb_[..., o:o + vw]; o += vw
    gr = hb_[..., o:o + vw]; o += vw
    beta_in, a_in, glr = hs_[..., :hg], hs_[..., hg:2 * hg], hs_[..., 2 * hg:2 * hg + rank]

    conv_w = lw["gdn_conv_w"]
    width = conv_w.shape[0]
    full = jnp.concatenate([conv_buf, qkv_pre], axis=1)
    acc = full[:, 0:length] * conv_w[0]
    for j in range(1, width):
        acc = acc + full[:, j:j + length] * conv_w[j]
    qkv = _silu(acc)
    new_conv = full[:, length:]

    c = _chunk_len(length)
    lp = -(-length // c) * c
    nc = lp // c
    q = _l2n(qkv[..., :gw].reshape(bsz, length, hg, dg)) * (dg ** -0.5)
    k = _l2n(qkv[..., gw:2 * gw].reshape(bsz, length, hg, dg))
    v = qkv[..., 2 * gw:]
    beta = jax.nn.sigmoid(beta_in)
    log_a = -jnp.exp(lw["gdn_a_log"]) * jax.nn.softplus(a_in + lw["gdn_dt_bias"])
    q = _pad_time(q.reshape(bsz, length, gw), lp)
    k = _pad_time(k.reshape(bsz, length, gw), lp)
    v = _pad_time(v, lp)
    beta = _pad_time(beta, lp)
    log_a = _pad_time(log_a, lp)
    hb = max(hg if c < CHUNK else min(hg, 4), LANES // c)
    bb = 4 if (c < CHUNK and bsz % 4 == 0) else 1
    g = jnp.cumsum(log_a.reshape(bsz, nc, c, hg), axis=2)

    def lane_rows(t):
        return jnp.swapaxes(t, 2, 3).reshape(bsz, nc, hg // hb, hb * c // LANES, LANES)

    o_gdn, s_gdn_new = _gdn_scan(q, k, v, lane_rows(g), lane_rows(beta.reshape(bsz, nc, c, hg)),
                                 s_gdn, c=c, bb=bb, hb=hb)
    o_gdn = o_gdn[:, :length].reshape(bsz, length, hg, dg)
    o_gdn = _rms(o_gdn, lw["gdn_norm_g"]) * _silu(z.reshape(bsz, length, hg, dg))

    lf = jax.nn.log_sigmoid(jnp.dot(glr, lw["gla_gate_w2"], precision=HI) + lw["gla_gate_b2"]) / GLA_GATE_TAU
    o_gla, s_gla_new = _gla_scan(_pad_time(gq * (dk ** -0.5), lp), _pad_time(gk, lp), _pad_time(gv, lp),
                                 _pad_time(lf, lp), s_gla, c=c, hb=1 if c == CHUNK else min(hl, 2))
    o_gla = o_gla[:, :length].reshape(bsz, length, hl, dv)
    o_gla = _rms(o_gla, lw["gla_norm_g"]) * _silu(gr.reshape(bsz, length, hl, dv))
    return (o_gdn.reshape(bsz * length, gw).astype(BF16), o_gla.reshape(bsz * length, vw).astype(BF16),
            new_conv, s_gdn_new, s_gla_new)


def _moe(x1, x1b, lw, layer, weights):
    n, d = x1.shape
    rg_w, re_w = lw["router_group_w"], lw["router_expert_w"]
    n_groups, n_exp = rg_w.shape[-1], re_w.shape[-1]
    per_group = n_exp // n_groups
    rw = jnp.concatenate([rg_w, re_w], axis=1)
    rw = jnp.pad(rw, ((0, 0), (0, LANES - rw.shape[1] % LANES)))
    logits = _matmul_f32(x1, rw, tm=_tile(n, 512, SUBLANES), name="router")
    group_logits = logits[:, :n_groups] + lw["router_group_b"]
    group_p = jax.nn.softmax(group_logits, axis=-1)
    g_sel = jnp.argmax(group_logits, axis=-1)
    p_sel = jnp.take_along_axis(group_p, g_sel[:, None], axis=1)
    exp_logits = (logits[:, n_groups:n_groups + n_exp] + lw["router_expert_b"]).reshape(n, n_groups, per_group)
    in_group = jnp.take_along_axis(exp_logits, g_sel[:, None, None], axis=1)[:, 0]
    top_v, top_i = lax.top_k(in_group, 2)
    wts = jax.nn.softmax(top_v, axis=-1) * p_sel
    expert_id = (g_sel[:, None] * per_group + top_i).astype(jnp.int32)

    tm = MOE_ROW_TILE
    flat_e = expert_id.reshape(-1)
    na = flat_e.shape[0]
    counts = jnp.zeros((n_exp,), jnp.int32).at[flat_e].add(1)
    tiles_per = (counts + tm - 1) // tm
    tile_end = jnp.cumsum(tiles_per)
    row_start = (tile_end - tiles_per) * tm
    order = jnp.argsort(flat_e, stable=True)
    sorted_e = flat_e[order]
    first = jnp.cumsum(counts) - counts
    rank = jnp.arange(na, dtype=jnp.int32) - first[sorted_e]
    dest_sorted = row_start[sorted_e] + rank
    n_rows_tiles = (na + n_exp * (tm - 1)) // tm + 1
    t_rows = n_rows_tiles * tm
    row_token = jnp.zeros((t_rows,), jnp.int32).at[dest_sorted].set(order // 2)
    dest = jnp.zeros((na,), jnp.int32).at[order].set(dest_sorted)
    n_tiles = tile_end[-1:]
    tile_ids = jnp.arange(n_rows_tiles, dtype=jnp.int32)
    tile_expert = jnp.searchsorted(tile_end, jnp.minimum(tile_ids, n_tiles[0] - 1), side="right").astype(jnp.int32)
    tile_expert = jnp.minimum(tile_expert, n_exp - 1)

    xs = x1b[row_token]
    fdim = weights["expert_w_gate"].shape[-1]
    ys = _moe_experts(xs, tile_expert, n_tiles.astype(jnp.int32), weights["expert_w_gate"],
                      weights["expert_w_up"], weights["expert_w_down"], layer=layer, tm=tm,
                      tf=_tile(fdim, 256, LANES))
    picked = ys[dest].reshape(n, 2, d)
    return picked[:, 0] * wts[:, 0:1] + picked[:, 1] * wts[:, 1:2]


def kernel(x_prompt, x_sample, state_gdn_conv, state_gdn, state_gla, w_in, gdn_conv_w, gdn_a_log, gdn_dt_bias,
           gdn_norm_g, gla_gate_w2, gla_gate_b2, gla_norm_g, w_proj_gdn, w_proj_gla, w_out, ln_mix_g, ln_mix_b,
           router_group_w, router_group_b, router_expert_w, router_expert_b,
           expert_w_gate, expert_w_up, expert_w_down, ln_ffn_g, ln_ffn_b):
    depth, d, _ = w_in.shape
    bp, lp_, _ = x_prompt.shape
    bs, ls_, _ = x_sample.shape
    hg, dg = gdn_a_log.shape[-1], gdn_norm_g.shape[-1]
    gw = hg * dg
    rank, kw = gla_gate_w2.shape[1], gla_gate_w2.shape[2]
    dv = gla_norm_g.shape[-1]
    vw = w_proj_gla.shape[1]
    hl = vw // dv
    dk = kw // hl
    width = gdn_conv_w.shape[1]
    alpha = (2 * depth) ** 0.25
    dims = (hg, dg, hl, dk, dv, rank)

    np_, ns_ = bp * lp_, bs * ls_
    n = np_ + ns_
    x = jnp.concatenate([x_prompt.reshape(np_, d), x_sample.reshape(ns_, d)], axis=0)
    xb = x.astype(BF16)

    sizes = (3 * gw, hg, hg, gw, kw, kw, vw, rank, vw, d, d)
    offs = [0]
    for sz in sizes:
        offs.append(offs[-1] + sz)
    big_order = (0, 3, 4, 5, 6, 8, 9, 10)
    small_order = (1, 2, 7)
    big_off = {}
    acc = 0
    for i in big_order:
        big_off[i] = acc
        acc += sizes[i]
    big_cols = acc
    ma_off, mb_off = big_off[9], big_off[10]
    small_cols = sum(sizes[i] for i in small_order)
    fused_prompt = (lp_ % CHUNK == 0 and lp_ % GLA_CHUNK == 0 and 2 * hg + rank <= LANES and width - 1 <= SUBLANES
                    and LANES % CHUNK == 0 and hg % (LANES // CHUNK) == 0
                    and big_off[4] % dk == 0 and big_off[5] % dk == 0
                    and big_off[6] % dv == 0 and big_off[8] % dv == 0)

    tm = _tile(n, 1088, SUBLANES)
    tr = _tile(n, 256, SUBLANES)
    zeros_conv = jnp.zeros((bp, width - 1, 3 * gw), F32)
    zeros_gdn = jnp.zeros((bp, hg, dg, dg), F32)
    zeros_gla = jnp.zeros((bp, hl, dk, dv), F32)

    outs = [[] for _ in range(6)]
    for l in range(depth):
        lw = dict(gdn_conv_w=gdn_conv_w[l], gdn_a_log=gdn_a_log[l], gdn_dt_bias=gdn_dt_bias[l],
                  gdn_norm_g=gdn_norm_g[l], gla_gate_w2=gla_gate_w2[l], gla_gate_b2=gla_gate_b2[l],
                  gla_norm_g=gla_norm_g[l], router_group_w=router_group_w[l], router_group_b=router_group_b[l],
                  router_expert_w=router_expert_w[l], router_expert_b=router_expert_b[l])
        wl = w_in[l]
        w_big = jnp.concatenate([wl[:, offs[i]:offs[i + 1]] for i in big_order], axis=1).astype(BF16)
        w_small = jnp.concatenate([wl[:, offs[i]:offs[i + 1]] for i in small_order], axis=1)
        w_small = jnp.pad(w_small, ((0, 0), (0, LANES - small_cols))).astype(BF16)

        h_big = _matmul(xb, w_big, tm=tm, tn=_tile(big_cols, 512, LANES), name="in_proj")
        h_small = _matmul(xb, w_small, tm=tm, tn=LANES, name="in_proj_small")

        if fused_prompt:
            og_p, gp, cq, ck, cv = _gdn_prompt(h_big, h_small, lw["gdn_conv_w"], lw["gdn_a_log"],
                                               lw["gdn_dt_bias"], lw["gdn_norm_g"],
                                               bsz=bp, length=lp_, hg=hg, dg=dg)
            cp = jnp.concatenate([t[:, SUBLANES - (width - 1):] for t in (cq, ck, cv)], axis=-1)
            w2_rows = jnp.pad(lw["gla_gate_w2"], ((2 * hg, LANES - 2 * hg - rank), (0, 0)))
            ol_p, sp = _gla_prompt(h_big, h_small, w2_rows, lw["gla_gate_b2"], lw["gla_norm_g"],
                                   (big_off[4], big_off[5], big_off[6], big_off[8]),
                                   bsz=bp, length=lp_, hl=hl, dk=dk, dv=dv)
        else:
            og_p, ol_p, cp, gp, sp = _mixer_group(h_big[:np_].reshape(bp, lp_, big_cols),
                                                  h_small[:np_].reshape(bp, lp_, LANES),
                                                  zeros_conv, zeros_gdn, zeros_gla, lw, dims)
        og_s, ol_s, cs, gs, ss = _mixer_group(h_big[np_:].reshape(bs, ls_, big_cols),
                                              h_small[np_:].reshape(bs, ls_, LANES),
                                              state_gdn_conv[l], state_gdn[l], state_gla[l], lw, dims)
        og = jnp.concatenate([og_p, og_s], axis=0)
        ol = jnp.concatenate([ol_p, ol_s], axis=0)
        merged = _merge_proj(og, ol, w_proj_gdn, w_proj_gla, h_big, ma_off, mb_off, layer=l, tm=tm,
                             tn=_tile(d, 256, LANES))
        mix = _matmul(merged, w_out, layer=l, tm=tm, tn=_tile(d, 512, LANES), name="out_proj")
        x1, x1b = _deepnorm_ln(x, mix, ln_mix_g[l], ln_mix_b[l], alpha=alpha, tr=tr)
        ffn = _moe(x1, x1b, lw, l, dict(expert_w_gate=expert_w_gate, expert_w_up=expert_w_up,
                                        expert_w_down=expert_w_down))
        x, xb = _deepnorm_ln(x1, ffn, ln_ffn_g[l], ln_ffn_b[l], alpha=alpha, tr=tr)
        for lst, val in zip(outs, (cp, gp, sp, cs, gs, ss)):
            lst.append(val)

    return (x[:np_].reshape(bp, lp_, d), x[np_:].reshape(bs, ls_, d)) + tuple(jnp.stack(o) for o in outs)
```

```python
import functools
import math

import jax
import jax.numpy as jnp
from jax import lax
from jax.experimental import pallas as pl
from jax.experimental.pallas import tpu as pltpu

F32 = jnp.float32
BF16 = jnp.bfloat16
HI = lax.Precision.HIGHEST

V7X_VMEM_LIMIT_BYTES = 58 * 1024 * 1024
LANES = 128
SUBLANES = 8

CHUNK = 64
GLA_CHUNK = 128
GLA_SUB = 16
LN_EPS = 1e-5
RMS_EPS = 1e-6
GLA_GATE_TAU = 16.0
MOE_TILE_SLACK = 1.15
MOE_MAX_ROW_TILE = 1024
MOE_K_TILE = 1024

NT = (((1,), (1,)), ((), ()))
TN = (((0,), (0,)), ((), ()))


def _params(*sem):
    return pltpu.CompilerParams(dimension_semantics=sem, vmem_limit_bytes=V7X_VMEM_LIMIT_BYTES)


def _tile(n, target, align):
    t = (min(target, n) // align) * align
    while t >= align:
        if n % t == 0:
            return t
        t -= align
    return n


def _drop_first(kernel_fn):
    def wrapped(_, *refs):
        kernel_fn(*refs)
    return wrapped


def _pallas_stacked(kernel_fn, prev, out_idx, *, in_specs, args, **kw):
    if prev is None:
        return pl.pallas_call(kernel_fn, in_specs=in_specs, **kw)(*args)
    return pl.pallas_call(_drop_first(kernel_fn), in_specs=[pl.BlockSpec(memory_space=pl.ANY)] + in_specs,
                          input_output_aliases={0: out_idx}, **kw)(prev, *args)


def _silu(x):
    return x * jax.nn.sigmoid(x)


def _l2n(x):
    return x * lax.rsqrt(jnp.sum(x * x, axis=-1, keepdims=True) + RMS_EPS)


def _rms(x, g):
    return x * lax.rsqrt(jnp.mean(x * x, axis=-1, keepdims=True) + RMS_EPS) * g


def _mm_kernel(a_ref, b_ref, o_ref):
    o_ref[...] = jnp.dot(a_ref[...], b_ref[...].astype(BF16),
                         preferred_element_type=F32).astype(o_ref.dtype)


def _matmul(a, b, *, layer=None, tm, tn, out_dtype=F32, name):
    m, k = a.shape
    n = b.shape[-1]
    if layer is None:
        b_spec = pl.BlockSpec((k, tn), lambda i, j: (0, j))
    else:
        b_spec = pl.BlockSpec((None, k, tn), lambda i, j: (layer, 0, j))
    return pl.pallas_call(
        _mm_kernel,
        grid=(m // tm, n // tn),
        in_specs=[pl.BlockSpec((tm, k), lambda i, j: (i, 0)), b_spec],
        out_specs=pl.BlockSpec((tm, tn), lambda i, j: (i, j)),
        out_shape=jax.ShapeDtypeStruct((m, n), out_dtype),
        compiler_params=_params("parallel", "parallel"),
        name=name,
    )(a, b)


def _mm_hi_kernel(a_ref, b_ref, o_ref):
    o_ref[...] = jnp.dot(a_ref[...], b_ref[...], precision=HI, preferred_element_type=F32)


def _matmul_f32(a, b, *, tm, name):
    m, k = a.shape
    n = b.shape[-1]
    return pl.pallas_call(
        _mm_hi_kernel,
        grid=(m // tm,),
        in_specs=[pl.BlockSpec((tm, k), lambda i: (i, 0)), pl.BlockSpec((k, n), lambda i: (0, 0))],
        out_specs=pl.BlockSpec((tm, n), lambda i: (i, 0)),
        out_shape=jax.ShapeDtypeStruct((m, n), F32),
        compiler_params=_params("parallel"),
        name=name,
    )(a, b)


def _merge_kernel(og_ref, ol_ref, wg_ref, wl_ref, ma_ref, mb_ref, o_ref):
    ya = jnp.dot(og_ref[...], wg_ref[...].astype(BF16), preferred_element_type=F32)
    yb = jnp.dot(ol_ref[...], wl_ref[...].astype(BF16), preferred_element_type=F32)
    o_ref[...] = (jax.nn.sigmoid(ma_ref[...]) * ya + jax.nn.sigmoid(mb_ref[...]) * yb).astype(o_ref.dtype)


def _merge_proj(og, ol, w_pg, w_pl, h, ma_off, mb_off, *, layer, tm, tn):
    m, kg = og.shape
    kl = ol.shape[1]
    d = w_pg.shape[-1]
    ja, jb = ma_off // tn, mb_off // tn
    return pl.pallas_call(
        _merge_kernel,
        grid=(m // tm, d // tn),
        in_specs=[
            pl.BlockSpec((tm, kg), lambda i, j: (i, 0)),
            pl.BlockSpec((tm, kl), lambda i, j: (i, 0)),
            pl.BlockSpec((None, kg, tn), lambda i, j: (layer, 0, j)),
            pl.BlockSpec((None, kl, tn), lambda i, j: (layer, 0, j)),
            pl.BlockSpec((tm, tn), lambda i, j: (i, ja + j)),
            pl.BlockSpec((tm, tn), lambda i, j: (i, jb + j)),
        ],
        out_specs=pl.BlockSpec((tm, tn), lambda i, j: (i, j)),
        out_shape=jax.ShapeDtypeStruct((m, d), BF16),
        compiler_params=_params("parallel", "parallel"),
        name="merge_proj",
    )(og, ol, w_pg, w_pl, h, h)


def _ln_store(y, g_ref, b_ref, o_ref, ob_ref):
    mu = jnp.mean(y, axis=-1, keepdims=True)
    yc = y - mu
    var = jnp.mean(yc * yc, axis=-1, keepdims=True)
    out = yc * lax.rsqrt(var + LN_EPS) * g_ref[...] + b_ref[...]
    o_ref[...] = out
    ob_ref[...] = out.astype(BF16)


def _ln_kernel(x_ref, r_ref, g_ref, b_ref, o_ref, ob_ref, *, alpha):
    _ln_store(alpha * x_ref[...] + r_ref[...], g_ref, b_ref, o_ref, ob_ref)


def _combine_ln_kernel(x_ref, ya_ref, yb_ref, w_ref, g_ref, b_ref, o_ref, ob_ref, *, alpha):
    w = w_ref[...]
    ffn = ya_ref[...].astype(F32) * w[:, 0:1] + yb_ref[...].astype(F32) * w[:, 1:2]
    _ln_store(alpha * x_ref[...] + ffn, g_ref, b_ref, o_ref, ob_ref)


def _ln_call(kernel_fn, rows, g, b, *, tr, name):
    n, d = rows[0].shape
    row = pl.BlockSpec((tr, d), lambda i: (i, 0))
    vec = pl.BlockSpec((1, d), lambda i: (0, 0))
    specs = [row if a.shape[1] == d else pl.BlockSpec((tr, a.shape[1]), lambda i: (i, 0)) for a in rows]
    return pl.pallas_call(
        kernel_fn,
        grid=(n // tr,),
        in_specs=specs + [vec, vec],
        out_specs=[row, row],
        out_shape=[jax.ShapeDtypeStruct((n, d), F32), jax.ShapeDtypeStruct((n, d), BF16)],
        compiler_params=_params("parallel"),
        name=name,
    )(*rows, g.reshape(1, d), b.reshape(1, d))


def _deepnorm_ln(x, r, g, b, *, alpha, tr):
    return _ln_call(functools.partial(_ln_kernel, alpha=alpha), [x, r], g, b, tr=tr, name="deepnorm_ln")


def _combine_ln(x, ya, yb, wts, g, b, *, alpha, tr):
    return _ln_call(functools.partial(_combine_ln_kernel, alpha=alpha), [x, ya, yb, wts], g, b, tr=tr,
                    name="combine_ln")


def _split2(a):
    hi = a.astype(BF16)
    return hi, (a - hi.astype(F32)).astype(BF16)


def _dot3(a, b, dims=None):
    def d(x, y):
        if dims is None:
            return jnp.dot(x, y, preferred_element_type=F32)
        return lax.dot_general(x, y, dims, preferred_element_type=F32)
    return d(a[0], b[0]) + (d(a[0], b[1]) + d(a[1], b[0]))


def _cumsum_rows(tri, x):
    hi = x.astype(BF16)
    r1 = x - hi.astype(F32)
    mid = r1.astype(BF16)
    lo = (r1 - mid.astype(F32)).astype(BF16)
    d = lambda y: jnp.dot(tri, y, preferred_element_type=F32)
    return d(hi) + (d(mid) + d(lo))


def _tri(c):
    r = lax.broadcasted_iota(jnp.int32, (c, c), 0)
    s = lax.broadcasted_iota(jnp.int32, (c, c), 1)
    return jnp.where(s <= r, 1.0, 0.0).astype(BF16)


def _pack_masks(c):
    lc = int(math.log2(c))
    r = lax.broadcasted_iota(jnp.int32, (LANES, LANES), 0)
    s = lax.broadcasted_iota(jnp.int32, (LANES, LANES), 1)
    same = lax.shift_right_logical(r, lc) == lax.shift_right_logical(s, lc)
    eye = r == s
    levels = []
    for lvl in range(1, lc + 1):
        rr = lax.shift_right_logical(r, lvl - 1)
        ss = lax.shift_right_logical(s, lvl - 1)
        levels.append(jnp.logical_and(rr == ss + 1, jnp.bitwise_and(ss, 1) == 0))
    return dict(eye=eye, causal=jnp.logical_and(same, s <= r), strict=jnp.logical_and(same, s < r),
                last=s == (lax.shift_left(lax.shift_right_logical(r, lc), lc) + (c - 1)),
                ident=jnp.where(eye, 1.0, 0.0).astype(F32), levels=levels)


def _unit_lower_inverse(a_list, mk):
    t = [mk["ident"] - jnp.where(mk["levels"][0], a, 0.0) for a in a_list]
    for m in mk["levels"][1:]:
        ts = [_split2(x) for x in t]
        tl = [_dot3(x, _split2(jnp.where(m, a, 0.0))) for x, a in zip(ts, a_list)]
        t = [x - _dot3(_split2(y), xs) for x, y, xs in zip(t, tl, ts)]
    return t


def _gdn_packs(packs, c, dg, mk):
    causal, strict = mk["causal"], mk["strict"]
    bdot = lambda x, y: jnp.dot(x.astype(BF16), y.astype(BF16), preferred_element_type=F32)
    decay = [jnp.exp(jnp.where(causal, p["gcol"] - p["grow"], -1e30)) for p in packs]
    ks = [_split2(p["k"]) for p in packs]
    kk = [_dot3(x, x, NT) for x in ks]
    a = [jnp.where(strict, p["bcol"] * y * d, 0.0) for p, y, d in zip(packs, kk, decay)]
    t = _unit_lower_inverse(a, mk)
    eg = [jnp.exp(p["gcol"]) for p in packs]
    rhs = [jnp.concatenate([p["bcol"] * p["v"], (p["bcol"] * e) * p["k"]], axis=1) for p, e in zip(packs, eg)]
    sol = [_dot3(_split2(x), _split2(y)) for x, y in zip(t, rhs)]
    qk = [lax.dot_general(p["q"].astype(BF16), x[0], NT, preferred_element_type=F32) for p, x in zip(packs, ks)]
    pm = [jnp.where(causal, x * d, 0.0) for x, d in zip(qk, decay)]
    qg = [p["q"] * e for p, e in zip(packs, eg)]
    rows = [slice(j * c, (j + 1) * c) for j in range(LANES // c)]
    res = [[bdot(jnp.concatenate([s[:, dg:][sl], g[sl]], axis=0), p["state"][h])
            for sl, h in zip(rows, p["heads"])] for p, s, g in zip(packs, sol, qg)]
    vnew = [jnp.concatenate([s[:, :dg][sl] - r[:c] for sl, r in zip(rows, rs)], axis=0)
            for s, rs in zip(sol, res)]
    out = [jnp.concatenate([r[c:] for r in rs], axis=0) + bdot(m, vn) for rs, m, vn in zip(res, pm, vnew)]
    kd = [p["k"] * jnp.exp(p["glast"] - p["gcol"]) for p in packs]
    ds = [[lax.dot_general(x[sl].astype(BF16), vn[sl].astype(BF16), TN, preferred_element_type=F32)
           for sl in rows] for x, vn in zip(kd, vnew)]
    for p, dsp in zip(packs, ds):
        for sl, h, d in zip(rows, p["heads"], dsp):
            p["state"][h] = p["state"][h] * jnp.exp(p["glast"][sl][0:1, :]) + d
    return out


def _gdn_kernel(q_ref, k_ref, v_ref, g_ref, b_ref, s0_ref, o_ref, so_ref, s_acc, *, c, bb, hb, dg):
    ci = pl.program_id(2)

    @pl.when(ci == 0)
    def _():
        s_acc[...] = s0_ref[...]

    per_pack = LANES // c
    mk = _pack_masks(c)

    def to_col(mask, row):
        return jnp.sum(jnp.where(mask, row, 0.0), axis=1, keepdims=True)

    packs = []
    for bi in range(bb):
        for p in range(hb // per_pack):
            heads = [p * per_pack + j for j in range(per_pack)]
            grow = g_ref[bi, p:p + 1, :]
            brow = b_ref[bi, p:p + 1, :]
            stack = lambda ref: jnp.concatenate([ref[bi, :, h * dg:(h + 1) * dg] for h in heads], axis=0)
            packs.append(dict(q=stack(q_ref), k=stack(k_ref), v=stack(v_ref), gcol=to_col(mk["eye"], grow),
                              grow=grow, bcol=to_col(mk["eye"], brow), glast=to_col(mk["last"], grow),
                              state=s_acc.at[bi], heads=heads, bi=bi))
    for p, out in zip(packs, _gdn_packs(packs, c, dg, mk)):
        for j, h in enumerate(p["heads"]):
            o_ref[p["bi"], :, h * dg:(h + 1) * dg] = out[j * c:(j + 1) * c]

    @pl.when(ci == pl.num_programs(2) - 1)
    def _():
        so_ref[...] = s_acc[...]


def _gdn_scan(q, k, v, grow, brow, s0, prev, *, s0_layer, layer, depth, c, bb, hb):
    bsz, lp, width = q.shape
    nheads, dg = s0.shape[2], s0.shape[3]
    nc = lp // c
    packs = hb * c // LANES
    seq = pl.BlockSpec((bb, c, hb * dg), lambda b, h, i: (b, i, h))
    vec = pl.BlockSpec((bb, None, None, packs, LANES), lambda b, h, i: (b, i, h, 0, 0))
    st_in = pl.BlockSpec((None, bb, hb, dg, dg), lambda b, h, i: (s0_layer, b, h, 0, 0))
    st_out = pl.BlockSpec((None, bb, hb, dg, dg), lambda b, h, i: (layer, b, h, 0, 0))
    return _pallas_stacked(
        functools.partial(_gdn_kernel, c=c, bb=bb, hb=hb, dg=dg), prev, 1,
        in_specs=[seq, seq, seq, vec, vec, st_in], args=(q, k, v, grow, brow, s0),
        grid=(bsz // bb, nheads // hb, nc),
        out_specs=[seq, st_out],
        out_shape=[jax.ShapeDtypeStruct((bsz, lp, width), F32),
                   jax.ShapeDtypeStruct((depth,) + s0.shape[1:], F32)],
        scratch_shapes=[pltpu.VMEM((bb, hb, dg, dg), F32)],
        compiler_params=_params("parallel", "parallel", "arbitrary"),
        name="gdn_scan",
    )


def _gdn_prompt_kernel(hq_ref, hk_ref, hv_ref, hz_ref, hs_ref, wq_ref, wk_ref, wv_ref, alog_ref, dtb_ref,
                       ng_ref, o_ref, so_ref, cq_ref, ck_ref, cv_ref, s_acc, pq, pk, pv, *, c, hg, dg, width):
    ci = pl.program_id(1)

    @pl.when(ci == 0)
    def _():
        s_acc[...] = jnp.zeros_like(s_acc)
        pq[...] = jnp.zeros_like(pq)
        pk[...] = jnp.zeros_like(pk)
        pv[...] = jnp.zeros_like(pv)

    per_pack = LANES // c
    mk = _pack_masks(c)
    hs = hs_ref[...]
    beta_all = jax.nn.sigmoid(hs)
    la_all = -jnp.exp(alog_ref[...]) * jax.nn.softplus(hs + dtb_ref[...])
    g_all = _cumsum_rows(_tri(c), la_all)

    def conv(h_ref, prev_ref, w_ref, cs):
        cur = h_ref[:, cs]
        ext = jnp.concatenate([prev_ref[:, cs], cur], axis=0)
        w = w_ref[:, cs]
        acc = cur * w[width - 1:width]
        for j in range(1, width):
            acc = acc + ext[SUBLANES - j:SUBLANES - j + c] * w[width - 1 - j:width - j]
        return _silu(acc)

    packs = []
    for p in range(hg // per_pack):
        heads = [p * per_pack + j for j in range(per_pack)]
        cols = [slice(h * dg, (h + 1) * dg) for h in heads]
        gcol = jnp.concatenate([g_all[:, hg + h:hg + h + 1] for h in heads], axis=0)
        packs.append(dict(
            q=jnp.concatenate([_l2n(conv(hq_ref, pq, wq_ref, cs)) * (dg ** -0.5) for cs in cols], axis=0),
            k=jnp.concatenate([_l2n(conv(hk_ref, pk, wk_ref, cs)) for cs in cols], axis=0),
            v=jnp.concatenate([conv(hv_ref, pv, wv_ref, cs) for cs in cols], axis=0),
            gcol=gcol, grow=jnp.sum(jnp.where(mk["eye"], gcol, 0.0), axis=0, keepdims=True),
            bcol=jnp.concatenate([beta_all[:, h:h + 1] for h in heads], axis=0),
            glast=jnp.concatenate([jnp.broadcast_to(g_all[c - 1:c, hg + h:hg + h + 1], (c, 1)) for h in heads],
                                  axis=0),
            state=s_acc, heads=heads, cols=cols))
    for p, out in zip(packs, _gdn_packs(packs, c, dg, mk)):
        for j, cs in enumerate(p["cols"]):
            o = out[j * c:(j + 1) * c]
            o_ref[:, cs] = (_rms(o, ng_ref[...]) * _silu(hz_ref[:, cs])).astype(o_ref.dtype)

    tail = slice(c - SUBLANES, c)
    pq[...] = hq_ref[tail, :]
    pk[...] = hk_ref[tail, :]
    pv[...] = hv_ref[tail, :]

    @pl.when(ci == pl.num_programs(1) - 1)
    def _():
        so_ref[...] = s_acc[...]
        cq_ref[...] = hq_ref[tail, :]
        ck_ref[...] = hk_ref[tail, :]
        cv_ref[...] = hv_ref[tail, :]


def _gdn_prompt(h_big, h_small, conv_w, a_log, dt_bias, norm_g, prev, *, layer, depth, bsz, length, hg, dg):
    gw = hg * dg
    c = CHUNK
    nc = length // c
    width = conv_w.shape[0]
    pad = lambda vrow: jnp.pad(vrow.reshape(1, hg), ((0, 0), (hg, LANES - 2 * hg)))
    col = lambda k: pl.BlockSpec((c, gw), lambda b, i: (b * nc + i, k))
    wcol = lambda k: pl.BlockSpec((width, gw), lambda b, i: (0, k))
    vec = pl.BlockSpec((1, LANES), lambda b, i: (0, 0))
    tail = pl.BlockSpec((None, SUBLANES, gw), lambda b, i: (b, 0, 0))
    tail_shape = jax.ShapeDtypeStruct((bsz, SUBLANES, gw), F32)
    return _pallas_stacked(
        functools.partial(_gdn_prompt_kernel, c=c, hg=hg, dg=dg, width=width), prev, 1,
        in_specs=[col(0), col(1), col(2), col(3),
                  pl.BlockSpec((c, LANES), lambda b, i: (b * nc + i, 0)),
                  wcol(0), wcol(1), wcol(2), vec, vec,
                  pl.BlockSpec((1, dg), lambda b, i: (0, 0))],
        args=(h_big, h_big, h_big, h_big, h_small, conv_w, conv_w, conv_w, pad(a_log), pad(dt_bias),
              norm_g.reshape(1, dg)),
        grid=(bsz, nc),
        out_specs=[pl.BlockSpec((c, gw), lambda b, i: (b * nc + i, 0)),
                   pl.BlockSpec((None, None, hg, dg, dg), lambda b, i: (layer, b, 0, 0, 0)),
                   tail, tail, tail],
        out_shape=[jax.ShapeDtypeStruct((bsz * length, gw), BF16),
                   jax.ShapeDtypeStruct((depth, bsz, hg, dg, dg), F32),
                   tail_shape, tail_shape, tail_shape],
        scratch_shapes=[pltpu.VMEM((hg, dg, dg), F32)] + [pltpu.VMEM((SUBLANES, gw), F32)] * 3,
        compiler_params=_params("parallel", "arbitrary"),
        name="gdn_prompt",
    )


def _gla_chunk(q, k, v, lf, sj, tri, c, sb):
    b = _cumsum_rows(tri, lf)
    inter = jnp.dot((q * jnp.exp(b)).astype(BF16), sj.astype(BF16), preferred_element_type=F32)
    blocks = []
    for i in range(c // sb):
        lo, hi = i * sb, (i + 1) * sb
        b0 = b[lo:lo + 1, :]
        qi = (q[lo:hi] * jnp.exp(b[lo:hi] - b0)).astype(BF16)
        kc = (k[:hi] * jnp.exp(b0 - b[:hi])).astype(BF16)
        sc = lax.dot_general(qi, kc, NT, preferred_element_type=F32)
        t_idx = lax.broadcasted_iota(jnp.int32, (sb, hi), 0) + lo
        s_idx = lax.broadcasted_iota(jnp.int32, (sb, hi), 1)
        sc = jnp.where(s_idx <= t_idx, sc, 0.0)
        blocks.append(jnp.dot(sc.astype(BF16), v[:hi].astype(BF16), preferred_element_type=F32))
    bt = b.T
    bl = bt[:, c - 1:c]
    kdt = (k.T * jnp.exp(bl - bt)).astype(BF16)
    s_new = sj * jnp.exp(bl) + jnp.dot(kdt, v.astype(BF16), preferred_element_type=F32)
    return inter + jnp.concatenate(blocks, axis=0), s_new


def _gla_kernel(q_ref, k_ref, v_ref, f_ref, s0_ref, o_ref, so_ref, s_acc, *, c, hb, dk, dv, sb):
    ci = pl.program_id(2)

    @pl.when(ci == 0)
    def _():
        s_acc[...] = s0_ref[...]

    tri = _tri(c)
    for j in range(hb):
        ks, vs = slice(j * dk, (j + 1) * dk), slice(j * dv, (j + 1) * dv)
        out, s_new = _gla_chunk(q_ref[:, ks], k_ref[:, ks], v_ref[:, vs], f_ref[:, ks], s_acc[j], tri, c, sb)
        o_ref[:, vs] = out
        s_acc[j] = s_new

    @pl.when(ci == pl.num_programs(2) - 1)
    def _():
        so_ref[...] = s_acc[...]


def _gla_scan(q, k, v, lf, s0, prev, *, s0_layer, layer, depth, c, hb):
    bsz, lp, _ = q.shape
    nheads, dk, dv = s0.shape[2], s0.shape[3], s0.shape[4]
    kseq = pl.BlockSpec((None, c, hb * dk), lambda b, h, i: (b, i, h))
    vseq = pl.BlockSpec((None, c, hb * dv), lambda b, h, i: (b, i, h))
    st_in = pl.BlockSpec((None, None, hb, dk, dv), lambda b, h, i: (s0_layer, b, h, 0, 0))
    st_out = pl.BlockSpec((None, None, hb, dk, dv), lambda b, h, i: (layer, b, h, 0, 0))
    return _pallas_stacked(
        functools.partial(_gla_kernel, c=c, hb=hb, dk=dk, dv=dv, sb=min(GLA_SUB, c)), prev, 1,
        in_specs=[kseq, kseq, vseq, kseq, st_in], args=(q, k, v, lf, s0),
        grid=(bsz, nheads // hb, lp // c),
        out_specs=[vseq, st_out],
        out_shape=[jax.ShapeDtypeStruct((bsz, lp, nheads * dv), F32),
                   jax.ShapeDtypeStruct((depth,) + s0.shape[1:], F32)],
        scratch_shapes=[pltpu.VMEM((hb, dk, dv), F32)],
        compiler_params=_params("parallel", "parallel", "arbitrary"),
        name="gla_scan",
    )


def _gla_prompt_kernel(hq_ref, hk_ref, hv_ref, hr_ref, hs_ref, w2_ref, b2_ref, ng_ref, o_ref, so_ref, s_acc,
                       *, c, dk, sb):
    ci = pl.program_id(2)

    @pl.when(ci == 0)
    def _():
        s_acc[...] = jnp.zeros_like(s_acc)

    gate = jnp.dot(hs_ref[...], w2_ref[...], precision=HI, preferred_element_type=F32) + b2_ref[...]
    lf = jax.nn.log_sigmoid(gate) * (1.0 / GLA_GATE_TAU)
    out, s_new = _gla_chunk(hq_ref[...] * (dk ** -0.5), hk_ref[...], hv_ref[...], lf, s_acc[...], _tri(c), c, sb)
    s_acc[...] = s_new
    o_ref[...] = (_rms(out, ng_ref[...]) * _silu(hr_ref[...])).astype(o_ref.dtype)

    @pl.when(ci == pl.num_programs(2) - 1)
    def _():
        so_ref[...] = s_new


def _gla_prompt(h_big, h_small, w2_rows, b2, norm_g, offs, prev, *, layer, depth, bsz, length, hl, dk, dv):
    c = GLA_CHUNK
    nc = length // c
    oq, ok, ov, og = offs
    kcol = lambda off: pl.BlockSpec((c, dk), lambda b, h, i: (b * nc + i, off // dk + h))
    vcol = lambda off: pl.BlockSpec((c, dv), lambda b, h, i: (b * nc + i, off // dv + h))
    return _pallas_stacked(
        functools.partial(_gla_prompt_kernel, c=c, dk=dk, sb=GLA_SUB), prev, 1,
        in_specs=[kcol(oq), kcol(ok), vcol(ov), vcol(og),
                  pl.BlockSpec((c, LANES), lambda b, h, i: (b * nc + i, 0)),
                  pl.BlockSpec((LANES, dk), lambda b, h, i: (0, h)),
                  pl.BlockSpec((1, dk), lambda b, h, i: (0, h)),
                  pl.BlockSpec((1, dv), lambda b, h, i: (0, 0))],
        args=(h_big, h_big, h_big, h_big, h_small, w2_rows, b2.reshape(1, hl * dk), norm_g.reshape(1, dv)),
        grid=(bsz, hl, nc),
        out_specs=[pl.BlockSpec((c, dv), lambda b, h, i: (b * nc + i, h)),
                   pl.BlockSpec((None, None, None, dk, dv), lambda b, h, i: (layer, b, h, 0, 0))],
        out_shape=[jax.ShapeDtypeStruct((bsz * length, hl * dv), BF16),
                   jax.ShapeDtypeStruct((depth, bsz, hl, dk, dv), F32)],
        scratch_shapes=[pltpu.VMEM((dk, dv), F32)],
        compiler_params=_params("parallel", "parallel", "arbitrary"),
        name="gla_prompt",
    )


def _moe_kernel(te_ref, nt_ref, x_ref, wg_ref, wu_ref, wd_ref, o_ref, g_acc, u_acc, hid, acc, *, nk, nf, tf):
    i, s = pl.program_id(0), pl.program_id(1)
    valid = i < nt_ref[0]

    @pl.when(s == 0)
    def _():
        g_acc[...] = jnp.zeros_like(g_acc)
        u_acc[...] = jnp.zeros_like(u_acc)
        acc[...] = jnp.zeros_like(acc)

    @pl.when(jnp.logical_and(valid, s < nk))
    def _():
        x = x_ref[...]
        g_acc[...] += jnp.dot(x, wg_ref[...].astype(BF16), preferred_element_type=F32)
        u_acc[...] += jnp.dot(x, wu_ref[...].astype(BF16), preferred_element_type=F32)

    @pl.when(jnp.logical_and(valid, s == nk - 1))
    def _():
        for f in range(nf):
            cols = slice(f * tf, (f + 1) * tf)
            hid[f] = (_silu(g_acc[:, cols]) * u_acc[:, cols]).astype(BF16)

    @pl.when(jnp.logical_and(valid, s >= nk))
    def _():
        acc[...] += jnp.dot(hid[s - nk], wd_ref[...].astype(BF16), preferred_element_type=F32)

    @pl.when(s == nk + nf - 1)
    def _():
        o_ref[...] = acc[...].astype(o_ref.dtype)


def _moe_experts(xs, tile_expert, n_tiles, w_gate, w_up, w_down, *, layer, tm, tk, tf):
    t, d = xs.shape
    fdim = w_gate.shape[-1]
    nk, nf = d // tk, fdim // tf

    def kstep(i, s, nt):
        return jnp.where(i < nt[0], jnp.minimum(s, nk - 1), nk - 1)

    def fstep(i, s, nt):
        return jnp.where(i < nt[0], jnp.maximum(s - nk, 0), nf - 1)

    grid_spec = pltpu.PrefetchScalarGridSpec(
        num_scalar_prefetch=2,
        grid=(t // tm, nk + nf),
        in_specs=[
            pl.BlockSpec((tm, tk), lambda i, s, te, nt: (jnp.minimum(i, nt[0] - 1), kstep(i, s, nt))),
            pl.BlockSpec((None, None, tk, fdim), lambda i, s, te, nt: (layer, te[i], kstep(i, s, nt), 0)),
            pl.BlockSpec((None, None, tk, fdim), lambda i, s, te, nt: (layer, te[i], kstep(i, s, nt), 0)),
            pl.BlockSpec((None, None, tf, d), lambda i, s, te, nt: (layer, te[i], fstep(i, s, nt), 0)),
        ],
        out_specs=pl.BlockSpec((tm, d), lambda i, s, te, nt: (i, 0)),
        scratch_shapes=[pltpu.VMEM((tm, fdim), F32), pltpu.VMEM((tm, fdim), F32),
                        pltpu.VMEM((nf, tm, tf), BF16), pltpu.VMEM((tm, d), F32)],
    )
    return pl.pallas_call(
        functools.partial(_moe_kernel, nk=nk, nf=nf, tf=tf),
        grid_spec=grid_spec,
        out_shape=jax.ShapeDtypeStruct((t, d), BF16),
        compiler_params=_params("arbitrary", "arbitrary"),
        name="moe_experts",
    )(tile_expert, n_tiles, xs, w_gate, w_up, w_down)


def _relayout_kernel(a_ref, b_ref, o_ref, *, bounds, deltas):
    j = pl.program_id(1)
    tn = o_ref.shape[1]
    lo = 0
    for hi, delta in zip(bounds, deltas):
        @pl.when(jnp.logical_and(j >= lo, j < hi))
        def _(delta=delta):
            if delta == 0:
                o_ref[...] = a_ref[...].astype(o_ref.dtype)
            else:
                cat = jnp.concatenate([a_ref[...], b_ref[...]], axis=1)
                o_ref[...] = cat[:, delta:delta + tn].astype(o_ref.dtype)
        lo = hi


def _w_relayout(w, *, layer, out_cols, bounds, deltas, tk, tn):
    k, cols = w.shape[1], w.shape[2]
    lane_blocks = tn // LANES
    last = (cols - 1) // LANES
    return pl.pallas_call(
        functools.partial(_relayout_kernel, bounds=bounds, deltas=deltas),
        grid=(k // tk, out_cols // tn),
        in_specs=[pl.BlockSpec((None, tk, tn), lambda i, j: (layer, i, j)),
                  pl.BlockSpec((None, tk, LANES), lambda i, j: (layer, i, jnp.minimum((j + 1) * lane_blocks, last)))],
        out_specs=pl.BlockSpec((tk, tn), lambda i, j: (i, j)),
        out_shape=jax.ShapeDtypeStruct((k, out_cols), BF16),
        compiler_params=_params("parallel", "parallel"),
        name="w_relayout",
    )(w, w)


def _chunk_len(length):
    c = SUBLANES
    while c < min(length, CHUNK):
        c *= 2
    return c


def _pad_time(x, lp):
    pad = lp - x.shape[1]
    if pad == 0:
        return x
    return jnp.pad(x, [(0, 0), (0, pad)] + [(0, 0)] * (x.ndim - 2))


def _mixer_group(hb_, hs_, conv_buf, s_gdn, s_gla, s0_layer, prev_gdn, prev_gla, layer, depth, lw, dims):
    (hg, dg, hl, dk, dv, rank) = dims
    bsz, length, _ = hb_.shape
    gw, kw, vw = hg * dg, hl * dk, hl * dv
    o = 0
    qkv_pre = hb_[..., o:o + 3 * gw]; o += 3 * gw
    z = hb_[..., o:o + gw]; o += gw
    gq = hb_[..., o:o + kw]; o += kw
    gk = hb_[..., o:o + kw]; o += kw
    gv = hb_[..., o:o + vw]; o += vw
    gr = hb_[..., o:o + vw]; o += vw
    beta_in, a_in, glr = hs_[..., :hg], hs_[..., hg:2 * hg], hs_[..., 2 * hg:2 * hg + rank]

    conv_w = lw["gdn_conv_w"]
    width = conv_w.shape[0]
    full = jnp.concatenate([conv_buf, qkv_pre], axis=1)
    acc = full[:, 0:length] * conv_w[0]
    for j in range(1, width):
        acc = acc + full[:, j:j + length] * conv_w[j]
    qkv = _silu(acc)
    new_conv = full[:, length:]

    c = _chunk_len(length)
    lp = -(-length // c) * c
    nc = lp // c
    q = _l2n(qkv[..., :gw].reshape(bsz, length, hg, dg)) * (dg ** -0.5)
    k = _l2n(qkv[..., gw:2 * gw].reshape(bsz, length, hg, dg))
    v = qkv[..., 2 * gw:]
    beta = jax.nn.sigmoid(beta_in)
    log_a = -jnp.exp(lw["gdn_a_log"]) * jax.nn.softplus(a_in + lw["gdn_dt_bias"])
    q = _pad_time(q.reshape(bsz, length, gw), lp)
    k = _pad_time(k.reshape(bsz, length, gw), lp)
    v = _pad_time(v, lp)
    beta = _pad_time(beta, lp)
    log_a = _pad_time(log_a, lp)
    hb = max(hg if c < CHUNK else min(hg, 4), LANES // c)
    bb = 4 if (c < CHUNK and bsz % 4 == 0) else 1
    g = jnp.cumsum(log_a.reshape(bsz, nc, c, hg), axis=2)

    def lane_rows(t):
        return jnp.swapaxes(t, 2, 3).reshape(bsz, nc, hg // hb, hb * c // LANES, LANES)

    o_gdn, s_gdn_new = _gdn_scan(q, k, v, lane_rows(g), lane_rows(beta.reshape(bsz, nc, c, hg)),
                                 s_gdn, prev_gdn, s0_layer=s0_layer, layer=layer, depth=depth, c=c, bb=bb, hb=hb)
    o_gdn = o_gdn[:, :length].reshape(bsz, length, hg, dg)
    o_gdn = _rms(o_gdn, lw["gdn_norm_g"]) * _silu(z.reshape(bsz, length, hg, dg))

    lf = jax.nn.log_sigmoid(jnp.dot(glr, lw["gla_gate_w2"], precision=HI) + lw["gla_gate_b2"]) / GLA_GATE_TAU
    o_gla, s_gla_new = _gla_scan(_pad_time(gq * (dk ** -0.5), lp), _pad_time(gk, lp), _pad_time(gv, lp),
                                 _pad_time(lf, lp), s_gla, prev_gla, s0_layer=s0_layer, layer=layer, depth=depth,
                                 c=c, hb=1 if c == CHUNK else min(hl, 2))
    o_gla = o_gla[:, :length].reshape(bsz, length, hl, dv)
    o_gla = _rms(o_gla, lw["gla_norm_g"]) * _silu(gr.reshape(bsz, length, hl, dv))
    return (o_gdn.reshape(bsz * length, gw).astype(BF16), o_gla.reshape(bsz * length, vw).astype(BF16),
            new_conv, s_gdn_new, s_gla_new)


def _moe(x1, x1b, lw, layer, weights):
    n, d = x1.shape
    rg_w, re_w = lw["router_group_w"], lw["router_expert_w"]
    n_groups, n_exp = rg_w.shape[-1], re_w.shape[-1]
    per_group = n_exp // n_groups
    rw = jnp.concatenate([rg_w, re_w], axis=1)
    rw = jnp.pad(rw, ((0, 0), (0, LANES - rw.shape[1] % LANES)))
    logits = _matmul_f32(x1, rw, tm=_tile(n, 512, SUBLANES), name="router")
    group_logits = logits[:, :n_groups] + lw["router_group_b"]
    group_p = jax.nn.softmax(group_logits, axis=-1)
    g_sel = jnp.argmax(group_logits, axis=-1)
    p_sel = jnp.take_along_axis(group_p, g_sel[:, None], axis=1)
    exp_logits = (logits[:, n_groups:n_groups + n_exp] + lw["router_expert_b"]).reshape(n, n_groups, per_group)
    in_group = jnp.take_along_axis(exp_logits, g_sel[:, None, None], axis=1)[:, 0]
    top_v, top_i = lax.top_k(in_group, 2)
    wts = jax.nn.softmax(top_v, axis=-1) * p_sel
    expert_id = (g_sel[:, None] * per_group + top_i).astype(jnp.int32)

    tm = min(MOE_MAX_ROW_TILE, -(-int(MOE_TILE_SLACK * 2 * n / n_exp) // LANES) * LANES)
    flat_e = expert_id.reshape(-1)
    na = flat_e.shape[0]
    counts = jnp.zeros((n_exp,), jnp.int32).at[flat_e].add(1)
    tiles_per = (counts + tm - 1) // tm
    tile_end = jnp.cumsum(tiles_per)
    row_start = (tile_end - tiles_per) * tm
    order = jnp.argsort(flat_e, stable=True)
    sorted_e = flat_e[order]
    first = jnp.cumsum(counts) - counts
    rank = jnp.arange(na, dtype=jnp.int32) - first[sorted_e]
    dest_sorted = row_start[sorted_e] + rank
    n_rows_tiles = (na + n_exp * (tm - 1)) // tm + 1
    t_rows = n_rows_tiles * tm
    row_token = jnp.zeros((t_rows,), jnp.int32).at[dest_sorted].set(order // 2)
    dest = jnp.zeros((na,), jnp.int32).at[order].set(dest_sorted)
    n_tiles = tile_end[-1:]
    tile_ids = jnp.arange(n_rows_tiles, dtype=jnp.int32)
    tile_expert = jnp.searchsorted(tile_end, jnp.minimum(tile_ids, n_tiles[0] - 1), side="right").astype(jnp.int32)
    tile_expert = jnp.minimum(tile_expert, n_exp - 1)

    xs = x1b[row_token]
    fdim = weights["expert_w_gate"].shape[-1]
    ys = _moe_experts(xs, tile_expert, n_tiles.astype(jnp.int32), weights["expert_w_gate"],
                      weights["expert_w_up"], weights["expert_w_down"], layer=layer, tm=tm,
                      tk=_tile(d, MOE_K_TILE, LANES), tf=_tile(fdim, 256, LANES))
    dest = dest.reshape(n, 2)
    return ys[dest[:, 0]], ys[dest[:, 1]], wts


def kernel(x_prompt, x_sample, state_gdn_conv, state_gdn, state_gla, w_in, gdn_conv_w, gdn_a_log, gdn_dt_bias,
           gdn_norm_g, gla_gate_w2, gla_gate_b2, gla_norm_g, w_proj_gdn, w_proj_gla, w_out, ln_mix_g, ln_mix_b,
           router_group_w, router_group_b, router_expert_w, router_expert_b,
           expert_w_gate, expert_w_up, expert_w_down, ln_ffn_g, ln_ffn_b):
    depth, d, _ = w_in.shape
    bp, lp_, _ = x_prompt.shape
    bs, ls_, _ = x_sample.shape
    hg, dg = gdn_a_log.shape[-1], gdn_norm_g.shape[-1]
    gw = hg * dg
    rank, kw = gla_gate_w2.shape[1], gla_gate_w2.shape[2]
    dv = gla_norm_g.shape[-1]
    vw = w_proj_gla.shape[1]
    hl = vw // dv
    dk = kw // hl
    width = gdn_conv_w.shape[1]
    alpha = (2 * depth) ** 0.25
    dims = (hg, dg, hl, dk, dv, rank)

    np_, ns_ = bp * lp_, bs * ls_
    n = np_ + ns_
    x = jnp.concatenate([x_prompt.reshape(np_, d), x_sample.reshape(ns_, d)], axis=0)
    xb = x.astype(BF16)

    sizes = (3 * gw, hg, hg, gw, kw, kw, vw, rank, vw, d, d)
    offs = [0]
    for sz in sizes:
        offs.append(offs[-1] + sz)
    big_order = (0, 3, 4, 5, 6, 8, 9, 10)
    small_order = (1, 2, 7)
    big_off = {}
    acc = 0
    for i in big_order:
        big_off[i] = acc
        acc += sizes[i]
    big_cols = acc
    ma_off, mb_off = big_off[9], big_off[10]
    small_cols = sum(sizes[i] for i in small_order)
    fused_prompt = (lp_ % CHUNK == 0 and lp_ % GLA_CHUNK == 0 and 2 * hg + rank <= LANES and width - 1 <= SUBLANES
                    and LANES % CHUNK == 0 and hg % (LANES // CHUNK) == 0
                    and big_off[4] % dk == 0 and big_off[5] % dk == 0
                    and big_off[6] % dv == 0 and big_off[8] % dv == 0)

    tm = _tile(n, 1088, SUBLANES)
    tr = _tile(n, 256, SUBLANES)
    tn_in = _tile(big_cols, 512, LANES)
    bounds = (sizes[0] // tn_in, big_off[8] // tn_in, big_cols // tn_in)
    deltas = (0, 2 * hg, 2 * hg + rank)
    pallas_relayout = sizes[0] % tn_in == 0 and big_off[8] % tn_in == 0 and deltas[-1] < LANES
    zeros_conv = jnp.zeros((bp, width - 1, 3 * gw), F32)
    zeros_gdn = jnp.zeros((1, bp, hg, dg, dg), F32)
    zeros_gla = jnp.zeros((1, bp, hl, dk, dv), F32)

    conv_p, conv_s = [], []
    gp = sp = gs = ss = None
    for l in range(depth):
        lw = dict(gdn_conv_w=gdn_conv_w[l], gdn_a_log=gdn_a_log[l], gdn_dt_bias=gdn_dt_bias[l],
                  gdn_norm_g=gdn_norm_g[l], gla_gate_w2=gla_gate_w2[l], gla_gate_b2=gla_gate_b2[l],
                  gla_norm_g=gla_norm_g[l], router_group_w=router_group_w[l], router_group_b=router_group_b[l],
                  router_expert_w=router_expert_w[l], router_expert_b=router_expert_b[l])
        wl = w_in[l]
        if pallas_relayout:
            w_big = _w_relayout(w_in, layer=l, out_cols=big_cols, bounds=bounds, deltas=deltas,
                                tk=_tile(d, 1024, SUBLANES), tn=tn_in)
        else:
            w_big = jnp.concatenate([wl[:, offs[i]:offs[i + 1]] for i in big_order], axis=1).astype(BF16)
        w_small = jnp.concatenate([wl[:, offs[i]:offs[i + 1]] for i in small_order], axis=1)
        w_small = jnp.pad(w_small, ((0, 0), (0, LANES - small_cols))).astype(BF16)

        h_big = _matmul(xb, w_big, tm=tm, tn=tn_in, name="in_proj")
        h_small = _matmul(xb, w_small, tm=tm, tn=LANES, name="in_proj_small")

        if fused_prompt:
            og_p, gp, cq, ck, cv = _gdn_prompt(h_big, h_small, lw["gdn_conv_w"], lw["gdn_a_log"],
                                               lw["gdn_dt_bias"], lw["gdn_norm_g"], gp, layer=l, depth=depth,
                                               bsz=bp, length=lp_, hg=hg, dg=dg)
            cp = jnp.concatenate([t[:, SUBLANES - (width - 1):] for t in (cq, ck, cv)], axis=-1)
            w2_rows = jnp.pad(lw["gla_gate_w2"], ((2 * hg, LANES - 2 * hg - rank), (0, 0)))
            ol_p, sp = _gla_prompt(h_big, h_small, w2_rows, lw["gla_gate_b2"], lw["gla_norm_g"],
                                   (big_off[4], big_off[5], big_off[6], big_off[8]), sp, layer=l, depth=depth,
                                   bsz=bp, length=lp_, hl=hl, dk=dk, dv=dv)
        else:
            og_p, ol_p, cp, gp, sp = _mixer_group(h_big[:np_].reshape(bp, lp_, big_cols),
                                                  h_small[:np_].reshape(bp, lp_, LANES),
                                                  zeros_conv, zeros_gdn, zeros_gla, 0, gp, sp, l, depth, lw, dims)
        og_s, ol_s, cs, gs, ss = _mixer_group(h_big[np_:].reshape(bs, ls_, big_cols),
                                              h_small[np_:].reshape(bs, ls_, LANES),
                                              state_gdn_conv[l], state_gdn, state_gla, l, gs, ss, l, depth, lw, dims)
        og = jnp.concatenate([og_p, og_s], axis=0)
        ol = jnp.concatenate([ol_p, ol_s], axis=0)
        merged = _merge_proj(og, ol, w_proj_gdn, w_proj_gla, h_big, ma_off, mb_off, layer=l, tm=tm,
                             tn=_tile(d, 256, LANES))
        mix = _matmul(merged, w_out, layer=l, tm=tm, tn=_tile(d, 512, LANES), name="out_proj")
        x1, x1b = _deepnorm_ln(x, mix, ln_mix_g[l], ln_mix_b[l], alpha=alpha, tr=tr)
        ya, yb, wts = _moe(x1, x1b, lw, l, dict(expert_w_gate=expert_w_gate, expert_w_up=expert_w_up,
                                               expert_w_down=expert_w_down))
        x, xb = _combine_ln(x1, ya, yb, wts, ln_ffn_g[l], ln_ffn_b[l], alpha=alpha, tr=tr)
        conv_p.append(cp)
        conv_s.append(cs)

    return (x[:np_].reshape(bp, lp_, d), x[np_:].reshape(bs, ls_, d),
            jnp.stack(conv_p), gp, sp, jnp.stack(conv_s), gs, ss)
```

```python
import functools
import math

import jax
import jax.numpy as jnp
from jax import lax
from jax.experimental import pallas as pl
from jax.experimental.pallas import tpu as pltpu

F32 = jnp.float32
BF16 = jnp.bfloat16
HI = lax.Precision.HIGHEST

V7X_VMEM_LIMIT_BYTES = 58 * 1024 * 1024
LANES = 128
SUBLANES = 8

CHUNK = 64
GLA_CHUNK = 128
GLA_SUB = 16
LN_EPS = 1e-5
RMS_EPS = 1e-6
GLA_GATE_TAU = 16.0
MOE_TILE_SLACK = 1.15
MOE_MAX_ROW_TILE = 1024
MOE_K_TILE = 1024

NT = (((1,), (1,)), ((), ()))
TN = (((0,), (0,)), ((), ()))


def _params(*sem):
    return pltpu.CompilerParams(dimension_semantics=sem, vmem_limit_bytes=V7X_VMEM_LIMIT_BYTES)


def _tile(n, target, align):
    t = (min(target, n) // align) * align
    while t >= align:
        if n % t == 0:
            return t
        t -= align
    return n


def _drop_first(kernel_fn):
    def wrapped(_, *refs):
        kernel_fn(*refs)
    return wrapped


def _pallas_stacked(kernel_fn, prev, out_idx, *, in_specs, args, **kw):
    if prev is None:
        return pl.pallas_call(kernel_fn, in_specs=in_specs, **kw)(*args)
    return pl.pallas_call(_drop_first(kernel_fn), in_specs=[pl.BlockSpec(memory_space=pl.ANY)] + in_specs,
                          input_output_aliases={0: out_idx}, **kw)(prev, *args)


def _silu(x):
    return x * jax.nn.sigmoid(x)


def _l2n(x):
    return x * lax.rsqrt(jnp.sum(x * x, axis=-1, keepdims=True) + RMS_EPS)


def _rms(x, g):
    return x * lax.rsqrt(jnp.mean(x * x, axis=-1, keepdims=True) + RMS_EPS) * g


def _mm_kernel(a_ref, b_ref, o_ref):
    o_ref[...] = jnp.dot(a_ref[...], b_ref[...].astype(BF16),
                         preferred_element_type=F32).astype(o_ref.dtype)


def _matmul(a, b, *, layer=None, tm, tn, out_dtype=F32, name):
    m, k = a.shape
    n = b.shape[-1]
    if layer is None:
        b_spec = pl.BlockSpec((k, tn), lambda i, j: (0, j))
    else:
        b_spec = pl.BlockSpec((None, k, tn), lambda i, j: (layer, 0, j))
    return pl.pallas_call(
        _mm_kernel,
        grid=(m // tm, n // tn),
        in_specs=[pl.BlockSpec((tm, k), lambda i, j: (i, 0)), b_spec],
        out_specs=pl.BlockSpec((tm, tn), lambda i, j: (i, j)),
        out_shape=jax.ShapeDtypeStruct((m, n), out_dtype),
        compiler_params=_params("parallel", "parallel"),
        name=name,
    )(a, b)


def _mm_hi_kernel(a_ref, b_ref, o_ref):
    o_ref[...] = jnp.dot(a_ref[...], b_ref[...], precision=HI, preferred_element_type=F32)


def _matmul_f32(a, b, *, tm, name):
    m, k = a.shape
    n = b.shape[-1]
    return pl.pallas_call(
        _mm_hi_kernel,
        grid=(m // tm,),
        in_specs=[pl.BlockSpec((tm, k), lambda i: (i, 0)), pl.BlockSpec((k, n), lambda i: (0, 0))],
        out_specs=pl.BlockSpec((tm, n), lambda i: (i, 0)),
        out_shape=jax.ShapeDtypeStruct((m, n), F32),
        compiler_params=_params("parallel"),
        name=name,
    )(a, b)


def _merge_kernel(og_ref, ol_ref, wg_ref, wl_ref, ma_ref, mb_ref, o_ref):
    ya = jnp.dot(og_ref[...], wg_ref[...].astype(BF16), preferred_element_type=F32)
    yb = jnp.dot(ol_ref[...], wl_ref[...].astype(BF16), preferred_element_type=F32)
    o_ref[...] = (jax.nn.sigmoid(ma_ref[...]) * ya + jax.nn.sigmoid(mb_ref[...]) * yb).astype(o_ref.dtype)


def _merge_proj(og, ol, w_pg, w_pl, h, ma_off, mb_off, *, layer, tm, tn):
    m, kg = og.shape
    kl = ol.shape[1]
    d = w_pg.shape[-1]
    ja, jb = ma_off // tn, mb_off // tn
    return pl.pallas_call(
        _merge_kernel,
        grid=(m // tm, d // tn),
        in_specs=[
            pl.BlockSpec((tm, kg), lambda i, j: (i, 0)),
            pl.BlockSpec((tm, kl), lambda i, j: (i, 0)),
            pl.BlockSpec((None, kg, tn), lambda i, j: (layer, 0, j)),
            pl.BlockSpec((None, kl, tn), lambda i, j: (layer, 0, j)),
            pl.BlockSpec((tm, tn), lambda i, j: (i, ja + j)),
            pl.BlockSpec((tm, tn), lambda i, j: (i, jb + j)),
        ],
        out_specs=pl.BlockSpec((tm, tn), lambda i, j: (i, j)),
        out_shape=jax.ShapeDtypeStruct((m, d), BF16),
        compiler_params=_params("parallel", "parallel"),
        name="merge_proj",
    )(og, ol, w_pg, w_pl, h, h)


def _ln_store(y, g_ref, b_ref, o_ref, ob_ref):
    mu = jnp.mean(y, axis=-1, keepdims=True)
    yc = y - mu
    var = jnp.mean(yc * yc, axis=-1, keepdims=True)
    out = yc * lax.rsqrt(var + LN_EPS) * g_ref[...] + b_ref[...]
    o_ref[...] = out
    ob_ref[...] = out.astype(BF16)


def _ln_kernel(x_ref, r_ref, g_ref, b_ref, o_ref, ob_ref, *, alpha):
    _ln_store(alpha * x_ref[...] + r_ref[...], g_ref, b_ref, o_ref, ob_ref)


def _combine_ln_kernel(x_ref, ya_ref, yb_ref, w_ref, g_ref, b_ref, o_ref, ob_ref, *, alpha):
    w = w_ref[...]
    ffn = ya_ref[...].astype(F32) * w[:, 0:1] + yb_ref[...].astype(F32) * w[:, 1:2]
    _ln_store(alpha * x_ref[...] + ffn, g_ref, b_ref, o_ref, ob_ref)


def _ln_call(kernel_fn, rows, g, b, *, tr, name):
    n, d = rows[0].shape
    row = pl.BlockSpec((tr, d), lambda i: (i, 0))
    vec = pl.BlockSpec((1, d), lambda i: (0, 0))
    specs = [row if a.shape[1] == d else pl.BlockSpec((tr, a.shape[1]), lambda i: (i, 0)) for a in rows]
    return pl.pallas_call(
        kernel_fn,
        grid=(n // tr,),
        in_specs=specs + [vec, vec],
        out_specs=[row, row],
        out_shape=[jax.ShapeDtypeStruct((n, d), F32), jax.ShapeDtypeStruct((n, d), BF16)],
        compiler_params=_params("parallel"),
        name=name,
    )(*rows, g.reshape(1, d), b.reshape(1, d))


def _deepnorm_ln(x, r, g, b, *, alpha, tr):
    return _ln_call(functools.partial(_ln_kernel, alpha=alpha), [x, r], g, b, tr=tr, name="deepnorm_ln")


def _combine_ln(x, ya, yb, wts, g, b, *, alpha, tr):
    return _ln_call(functools.partial(_combine_ln_kernel, alpha=alpha), [x, ya, yb, wts], g, b, tr=tr,
                    name="combine_ln")


def _split2(a):
    hi = a.astype(BF16)
    return hi, (a - hi.astype(F32)).astype(BF16)


def _dot3(a, b, dims=None):
    def d(x, y):
        if dims is None:
            return jnp.dot(x, y, preferred_element_type=F32)
        return lax.dot_general(x, y, dims, preferred_element_type=F32)
    return d(a[0], b[0]) + (d(a[0], b[1]) + d(a[1], b[0]))


def _cumsum_rows(tri, x):
    hi = x.astype(BF16)
    r1 = x - hi.astype(F32)
    mid = r1.astype(BF16)
    lo = (r1 - mid.astype(F32)).astype(BF16)
    d = lambda y: jnp.dot(tri, y, preferred_element_type=F32)
    return d(hi) + (d(mid) + d(lo))


def _tri(c):
    r = lax.broadcasted_iota(jnp.int32, (c, c), 0)
    s = lax.broadcasted_iota(jnp.int32, (c, c), 1)
    return jnp.where(s <= r, 1.0, 0.0).astype(BF16)


def _pack_masks(c):
    lc = int(math.log2(c))
    r = lax.broadcasted_iota(jnp.int32, (LANES, LANES), 0)
    s = lax.broadcasted_iota(jnp.int32, (LANES, LANES), 1)
    same = lax.shift_right_logical(r, lc) == lax.shift_right_logical(s, lc)
    eye = r == s
    levels = []
    for lvl in range(1, lc + 1):
        rr = lax.shift_right_logical(r, lvl - 1)
        ss = lax.shift_right_logical(s, lvl - 1)
        levels.append(jnp.logical_and(rr == ss + 1, jnp.bitwise_and(ss, 1) == 0))
    return dict(eye=eye, causal=jnp.logical_and(same, s <= r), strict=jnp.logical_and(same, s < r),
                last=s == (lax.shift_left(lax.shift_right_logical(r, lc), lc) + (c - 1)),
                ident=jnp.where(eye, 1.0, 0.0).astype(F32), levels=levels)


def _unit_lower_inverse(a_list, mk):
    t = [mk["ident"] - jnp.where(mk["levels"][0], a, 0.0) for a in a_list]
    for m in mk["levels"][1:]:
        ts = [_split2(x) for x in t]
        tl = [_dot3(x, _split2(jnp.where(m, a, 0.0))) for x, a in zip(ts, a_list)]
        t = [x - _dot3(_split2(y), xs) for x, y, xs in zip(t, tl, ts)]
    return t


def _gdn_packs(packs, c, dg, mk):
    causal, strict = mk["causal"], mk["strict"]
    bdot = lambda x, y: jnp.dot(x.astype(BF16), y.astype(BF16), preferred_element_type=F32)
    decay = [jnp.exp(jnp.where(causal, p["gcol"] - p["grow"], -1e30)) for p in packs]
    ks = [_split2(p["k"]) for p in packs]
    kk = [_dot3(x, x, NT) for x in ks]
    a = [jnp.where(strict, p["bcol"] * y * d, 0.0) for p, y, d in zip(packs, kk, decay)]
    t = _unit_lower_inverse(a, mk)
    eg = [jnp.exp(p["gcol"]) for p in packs]
    rhs = [jnp.concatenate([p["bcol"] * p["v"], (p["bcol"] * e) * p["k"]], axis=1) for p, e in zip(packs, eg)]
    sol = [_dot3(_split2(x), _split2(y)) for x, y in zip(t, rhs)]
    qk = [lax.dot_general(p["q"].astype(BF16), x[0], NT, preferred_element_type=F32) for p, x in zip(packs, ks)]
    pm = [jnp.where(causal, x * d, 0.0) for x, d in zip(qk, decay)]
    qg = [p["q"] * e for p, e in zip(packs, eg)]
    rows = [slice(j * c, (j + 1) * c) for j in range(LANES // c)]
    res = [[bdot(jnp.concatenate([s[:, dg:][sl], g[sl]], axis=0), p["state"][h])
            for sl, h in zip(rows, p["heads"])] for p, s, g in zip(packs, sol, qg)]
    vnew = [jnp.concatenate([s[:, :dg][sl] - r[:c] for sl, r in zip(rows, rs)], axis=0)
            for s, rs in zip(sol, res)]
    out = [jnp.concatenate([r[c:] for r in rs], axis=0) + bdot(m, vn) for rs, m, vn in zip(res, pm, vnew)]
    kd = [p["k"] * jnp.exp(p["glast"] - p["gcol"]) for p in packs]
    ds = [[lax.dot_general(x[sl].astype(BF16), vn[sl].astype(BF16), TN, preferred_element_type=F32)
           for sl in rows] for x, vn in zip(kd, vnew)]
    for p, dsp in zip(packs, ds):
        for sl, h, d in zip(rows, p["heads"], dsp):
            p["state"][h] = p["state"][h] * jnp.exp(p["glast"][sl][0:1, :]) + d
    return out


def _gdn_kernel(q_ref, k_ref, v_ref, g_ref, b_ref, s0_ref, o_ref, so_ref, s_acc, *, c, bb, hb, dg):
    ci = pl.program_id(2)

    @pl.when(ci == 0)
    def _():
        s_acc[...] = s0_ref[...]

    per_pack = LANES // c
    mk = _pack_masks(c)

    def to_col(mask, row):
        return jnp.sum(jnp.where(mask, row, 0.0), axis=1, keepdims=True)

    packs = []
    for bi in range(bb):
        for p in range(hb // per_pack):
            heads = [p * per_pack + j for j in range(per_pack)]
            grow = g_ref[bi, p:p + 1, :]
            brow = b_ref[bi, p:p + 1, :]
            stack = lambda ref: jnp.concatenate([ref[bi, :, h * dg:(h + 1) * dg] for h in heads], axis=0)
            packs.append(dict(q=stack(q_ref), k=stack(k_ref), v=stack(v_ref), gcol=to_col(mk["eye"], grow),
                              grow=grow, bcol=to_col(mk["eye"], brow), glast=to_col(mk["last"], grow),
                              state=s_acc.at[bi], heads=heads, bi=bi))
    for p, out in zip(packs, _gdn_packs(packs, c, dg, mk)):
        for j, h in enumerate(p["heads"]):
            o_ref[p["bi"], :, h * dg:(h + 1) * dg] = out[j * c:(j + 1) * c]

    @pl.when(ci == pl.num_programs(2) - 1)
    def _():
        so_ref[...] = s_acc[...]


def _gdn_scan(q, k, v, grow, brow, s0, prev, *, s0_layer, layer, depth, c, bb, hb):
    bsz, lp, width = q.shape
    nheads, dg = s0.shape[2], s0.shape[3]
    nc = lp // c
    packs = hb * c // LANES
    seq = pl.BlockSpec((bb, c, hb * dg), lambda b, h, i: (b, i, h))
    vec = pl.BlockSpec((bb, None, None, packs, LANES), lambda b, h, i: (b, i, h, 0, 0))
    st_in = pl.BlockSpec((None, bb, hb, dg, dg), lambda b, h, i: (s0_layer, b, h, 0, 0))
    st_out = pl.BlockSpec((None, bb, hb, dg, dg), lambda b, h, i: (layer, b, h, 0, 0))
    return _pallas_stacked(
        functools.partial(_gdn_kernel, c=c, bb=bb, hb=hb, dg=dg), prev, 1,
        in_specs=[seq, seq, seq, vec, vec, st_in], args=(q, k, v, grow, brow, s0),
        grid=(bsz // bb, nheads // hb, nc),
        out_specs=[seq, st_out],
        out_shape=[jax.ShapeDtypeStruct((bsz, lp, width), F32),
                   jax.ShapeDtypeStruct((depth,) + s0.shape[1:], F32)],
        scratch_shapes=[pltpu.VMEM((bb, hb, dg, dg), F32)],
        compiler_params=_params("parallel", "parallel", "arbitrary"),
        name="gdn_scan",
    )


def _gdn_prompt_kernel(hq_ref, hk_ref, hv_ref, hz_ref, hs_ref, wq_ref, wk_ref, wv_ref, alog_ref, dtb_ref,
                       ng_ref, o_ref, so_ref, cq_ref, ck_ref, cv_ref, s_acc, pq, pk, pv, *, c, hg, dg, width):
    ci = pl.program_id(1)

    @pl.when(ci == 0)
    def _():
        s_acc[...] = jnp.zeros_like(s_acc)
        pq[...] = jnp.zeros_like(pq)
        pk[...] = jnp.zeros_like(pk)
        pv[...] = jnp.zeros_like(pv)

    per_pack = LANES // c
    mk = _pack_masks(c)
    hs = hs_ref[...]
    beta_all = jax.nn.sigmoid(hs)
    la_all = -jnp.exp(alog_ref[...]) * jax.nn.softplus(hs + dtb_ref[...])
    g_all = _cumsum_rows(_tri(c), la_all)

    def conv(h_ref, prev_ref, w_ref, cs):
        cur = h_ref[:, cs]
        ext = jnp.concatenate([prev_ref[:, cs], cur], axis=0)
        w = w_ref[:, cs]
        acc = cur * w[width - 1:width]
        for j in range(1, width):
            acc = acc + ext[SUBLANES - j:SUBLANES - j + c] * w[width - 1 - j:width - j]
        return _silu(acc)

    packs = []
    for p in range(hg // per_pack):
        heads = [p * per_pack + j for j in range(per_pack)]
        cols = [slice(h * dg, (h + 1) * dg) for h in heads]
        gcol = jnp.concatenate([g_all[:, hg + h:hg + h + 1] for h in heads], axis=0)
        packs.append(dict(
            q=jnp.concatenate([_l2n(conv(hq_ref, pq, wq_ref, cs)) * (dg ** -0.5) for cs in cols], axis=0),
            k=jnp.concatenate([_l2n(conv(hk_ref, pk, wk_ref, cs)) for cs in cols], axis=0),
            v=jnp.concatenate([conv(hv_ref, pv, wv_ref, cs) for cs in cols], axis=0),
            gcol=gcol, grow=jnp.sum(jnp.where(mk["eye"], gcol, 0.0), axis=0, keepdims=True),
            bcol=jnp.concatenate([beta_all[:, h:h + 1] for h in heads], axis=0),
            glast=jnp.concatenate([jnp.broadcast_to(g_all[c - 1:c, hg + h:hg + h + 1], (c, 1)) for h in heads],
                                  axis=0),
            state=s_acc, heads=heads, cols=cols))
    for p, out in zip(packs, _gdn_packs(packs, c, dg, mk)):
        for j, cs in enumerate(p["cols"]):
            o = out[j * c:(j + 1) * c]
            o_ref[:, cs] = (_rms(o, ng_ref[...]) * _silu(hz_ref[:, cs])).astype(o_ref.dtype)

    tail = slice(c - SUBLANES, c)
    pq[...] = hq_ref[tail, :]
    pk[...] = hk_ref[tail, :]
    pv[...] = hv_ref[tail, :]

    @pl.when(ci == pl.num_programs(1) - 1)
    def _():
        so_ref[...] = s_acc[...]
        cq_ref[...] = hq_ref[tail, :]
        ck_ref[...] = hk_ref[tail, :]
        cv_ref[...] = hv_ref[tail, :]


def _gdn_prompt(h_big, h_small, conv_w, a_log, dt_bias, norm_g, prev, *, layer, depth, bsz, length, hg, dg):
    gw = hg * dg
    c = CHUNK
    nc = length // c
    width = conv_w.shape[0]
    pad = lambda vrow: jnp.pad(vrow.reshape(1, hg), ((0, 0), (hg, LANES - 2 * hg)))
    col = lambda k: pl.BlockSpec((c, gw), lambda b, i: (b * nc + i, k))
    wcol = lambda k: pl.BlockSpec((width, gw), lambda b, i: (0, k))
    vec = pl.BlockSpec((1, LANES), lambda b, i: (0, 0))
    tail = pl.BlockSpec((None, SUBLANES, gw), lambda b, i: (b, 0, 0))
    tail_shape = jax.ShapeDtypeStruct((bsz, SUBLANES, gw), F32)
    return _pallas_stacked(
        functools.partial(_gdn_prompt_kernel, c=c, hg=hg, dg=dg, width=width), prev, 1,
        in_specs=[col(0), col(1), col(2), col(3),
                  pl.BlockSpec((c, LANES), lambda b, i: (b * nc + i, 0)),
                  wcol(0), wcol(1), wcol(2), vec, vec,
                  pl.BlockSpec((1, dg), lambda b, i: (0, 0))],
        args=(h_big, h_big, h_big, h_big, h_small, conv_w, conv_w, conv_w, pad(a_log), pad(dt_bias),
              norm_g.reshape(1, dg)),
        grid=(bsz, nc),
        out_specs=[pl.BlockSpec((c, gw), lambda b, i: (b * nc + i, 0)),
                   pl.BlockSpec((None, None, hg, dg, dg), lambda b, i: (layer, b, 0, 0, 0)),
                   tail, tail, tail],
        out_shape=[jax.ShapeDtypeStruct((bsz * length, gw), BF16),
                   jax.ShapeDtypeStruct((depth, bsz, hg, dg, dg), F32),
                   tail_shape, tail_shape, tail_shape],
        scratch_shapes=[pltpu.VMEM((hg, dg, dg), F32)] + [pltpu.VMEM((SUBLANES, gw), F32)] * 3,
        compiler_params=_params("parallel", "arbitrary"),
        name="gdn_prompt",
    )


def _gla_chunk(q, k, v, lf, sj, tri, c, sb):
    b = _cumsum_rows(tri, lf)
    inter = jnp.dot((q * jnp.exp(b)).astype(BF16), sj.astype(BF16), preferred_element_type=F32)
    blocks = []
    for i in range(c // sb):
        lo, hi = i * sb, (i + 1) * sb
        b0 = b[lo:lo + 1, :]
        qi = (q[lo:hi] * jnp.exp(b[lo:hi] - b0)).astype(BF16)
        kc = (k[:hi] * jnp.exp(b0 - b[:hi])).astype(BF16)
        sc = lax.dot_general(qi, kc, NT, preferred_element_type=F32)
        t_idx = lax.broadcasted_iota(jnp.int32, (sb, hi), 0) + lo
        s_idx = lax.broadcasted_iota(jnp.int32, (sb, hi), 1)
        sc = jnp.where(s_idx <= t_idx, sc, 0.0)
        blocks.append(jnp.dot(sc.astype(BF16), v[:hi].astype(BF16), preferred_element_type=F32))
    bt = b.T
    bl = bt[:, c - 1:c]
    kdt = (k.T * jnp.exp(bl - bt)).astype(BF16)
    s_new = sj * jnp.exp(bl) + jnp.dot(kdt, v.astype(BF16), preferred_element_type=F32)
    return inter + jnp.concatenate(blocks, axis=0), s_new


def _gla_kernel(q_ref, k_ref, v_ref, f_ref, s0_ref, o_ref, so_ref, s_acc, *, c, hb, dk, dv, sb):
    ci = pl.program_id(2)

    @pl.when(ci == 0)
    def _():
        s_acc[...] = s0_ref[...]

    tri = _tri(c)
    for j in range(hb):
        ks, vs = slice(j * dk, (j + 1) * dk), slice(j * dv, (j + 1) * dv)
        out, s_new = _gla_chunk(q_ref[:, ks], k_ref[:, ks], v_ref[:, vs], f_ref[:, ks], s_acc[j], tri, c, sb)
        o_ref[:, vs] = out
        s_acc[j] = s_new

    @pl.when(ci == pl.num_programs(2) - 1)
    def _():
        so_ref[...] = s_acc[...]


def _gla_scan(q, k, v, lf, s0, prev, *, s0_layer, layer, depth, c, hb):
    bsz, lp, _ = q.shape
    nheads, dk, dv = s0.shape[2], s0.shape[3], s0.shape[4]
    kseq = pl.BlockSpec((None, c, hb * dk), lambda b, h, i: (b, i, h))
    vseq = pl.BlockSpec((None, c, hb * dv), lambda b, h, i: (b, i, h))
    st_in = pl.BlockSpec((None, None, hb, dk, dv), lambda b, h, i: (s0_layer, b, h, 0, 0))
    st_out = pl.BlockSpec((None, None, hb, dk, dv), lambda b, h, i: (layer, b, h, 0, 0))
    return _pallas_stacked(
        functools.partial(_gla_kernel, c=c, hb=hb, dk=dk, dv=dv, sb=min(GLA_SUB, c)), prev, 1,
        in_specs=[kseq, kseq, vseq, kseq, st_in], args=(q, k, v, lf, s0),
        grid=(bsz, nheads // hb, lp // c),
        out_specs=[vseq, st_out],
        out_shape=[jax.ShapeDtypeStruct((bsz, lp, nheads * dv), F32),
                   jax.ShapeDtypeStruct((depth,) + s0.shape[1:], F32)],
        scratch_shapes=[pltpu.VMEM((hb, dk, dv), F32)],
        compiler_params=_params("parallel", "parallel", "arbitrary"),
        name="gla_scan",
    )


def _gla_prompt_kernel(hq_ref, hk_ref, hv_ref, hr_ref, hs_ref, w2_ref, b2_ref, ng_ref, o_ref, so_ref, s_acc,
                       *, c, dk, sb):
    ci = pl.program_id(2)

    @pl.when(ci == 0)
    def _():
        s_acc[...] = jnp.zeros_like(s_acc)

    gate = jnp.dot(hs_ref[...], w2_ref[...], precision=HI, preferred_element_type=F32) + b2_ref[...]
    lf = jax.nn.log_sigmoid(gate) * (1.0 / GLA_GATE_TAU)
    out, s_new = _gla_chunk(hq_ref[...] * (dk ** -0.5), hk_ref[...], hv_ref[...], lf, s_acc[...], _tri(c), c, sb)
    s_acc[...] = s_new
    o_ref[...] = (_rms(out, ng_ref[...]) * _silu(hr_ref[...])).astype(o_ref.dtype)

    @pl.when(ci == pl.num_programs(2) - 1)
    def _():
        so_ref[...] = s_new


def _gla_prompt(h_big, h_small, w2_rows, b2, norm_g, offs, prev, *, layer, depth, bsz, length, hl, dk, dv):
    c = GLA_CHUNK
    nc = length // c
    oq, ok, ov, og = offs
    kcol = lambda off: pl.BlockSpec((c, dk), lambda b, h, i: (b * nc + i, off // dk + h))
    vcol = lambda off: pl.BlockSpec((c, dv), lambda b, h, i: (b * nc + i, off // dv + h))
    return _pallas_stacked(
        functools.partial(_gla_prompt_kernel, c=c, dk=dk, sb=GLA_SUB), prev, 1,
        in_specs=[kcol(oq), kcol(ok), vcol(ov), vcol(og),
                  pl.BlockSpec((c, LANES), lambda b, h, i: (b * nc + i, 0)),
                  pl.BlockSpec((LANES, dk), lambda b, h, i: (0, h)),
                  pl.BlockSpec((1, dk), lambda b, h, i: (0, h)),
                  pl.BlockSpec((1, dv), lambda b, h, i: (0, 0))],
        args=(h_big, h_big, h_big, h_big, h_small, w2_rows, b2.reshape(1, hl * dk), norm_g.reshape(1, dv)),
        grid=(bsz, hl, nc),
        out_specs=[pl.BlockSpec((c, dv), lambda b, h, i: (b * nc + i, h)),
                   pl.BlockSpec((None, None, None, dk, dv), lambda b, h, i: (layer, b, h, 0, 0))],
        out_shape=[jax.ShapeDtypeStruct((bsz * length, hl * dv), BF16),
                   jax.ShapeDtypeStruct((depth, bsz, hl, dk, dv), F32)],
        scratch_shapes=[pltpu.VMEM((dk, dv), F32)],
        compiler_params=_params("parallel", "parallel", "arbitrary"),
        name="gla_prompt",
    )


def _moe_kernel(te_ref, nt_ref, x_ref, wg_ref, wu_ref, wd_ref, o_ref, g_acc, u_acc, hid, acc, *, nk, nf, tf):
    i, s = pl.program_id(0), pl.program_id(1)
    valid = i < nt_ref[0]

    @pl.when(s == 0)
    def _():
        g_acc[...] = jnp.zeros_like(g_acc)
        u_acc[...] = jnp.zeros_like(u_acc)
        acc[...] = jnp.zeros_like(acc)

    @pl.when(jnp.logical_and(valid, s < nk))
    def _():
        x = x_ref[...]
        g_acc[...] += jnp.dot(x, wg_ref[...].astype(BF16), preferred_element_type=F32)
        u_acc[...] += jnp.dot(x, wu_ref[...].astype(BF16), preferred_element_type=F32)

    @pl.when(jnp.logical_and(valid, s == nk - 1))
    def _():
        for f in range(nf):
            cols = slice(f * tf, (f + 1) * tf)
            hid[f] = (_silu(g_acc[:, cols]) * u_acc[:, cols]).astype(BF16)

    @pl.when(jnp.logical_and(valid, s >= nk))
    def _():
        acc[...] += jnp.dot(hid[s - nk], wd_ref[...].astype(BF16), preferred_element_type=F32)

    @pl.when(s == nk + nf - 1)
    def _():
        o_ref[...] = acc[...].astype(o_ref.dtype)


def _moe_experts(xs, tile_expert, n_tiles, w_gate, w_up, w_down, *, layer, tm, tk, tf):
    t, d = xs.shape
    fdim = w_gate.shape[-1]
    nk, nf = d // tk, fdim // tf

    def kstep(i, s, nt):
        return jnp.where(i < nt[0], jnp.minimum(s, nk - 1), nk - 1)

    def fstep(i, s, nt):
        return jnp.where(i < nt[0], jnp.maximum(s - nk, 0), nf - 1)

    grid_spec = pltpu.PrefetchScalarGridSpec(
        num_scalar_prefetch=2,
        grid=(t // tm, nk + nf),
        in_specs=[
            pl.BlockSpec((tm, tk), lambda i, s, te, nt: (jnp.minimum(i, nt[0] - 1), kstep(i, s, nt))),
            pl.BlockSpec((None, None, tk, fdim), lambda i, s, te, nt: (layer, te[i], kstep(i, s, nt), 0)),
            pl.BlockSpec((None, None, tk, fdim), lambda i, s, te, nt: (layer, te[i], kstep(i, s, nt), 0)),
            pl.BlockSpec((None, None, tf, d), lambda i, s, te, nt: (layer, te[i], fstep(i, s, nt), 0)),
        ],
        out_specs=pl.BlockSpec((tm, d), lambda i, s, te, nt: (i, 0)),
        scratch_shapes=[pltpu.VMEM((tm, fdim), F32), pltpu.VMEM((tm, fdim), F32),
                        pltpu.VMEM((nf, tm, tf), BF16), pltpu.VMEM((tm, d), F32)],
    )
    return pl.pallas_call(
        functools.partial(_moe_kernel, nk=nk, nf=nf, tf=tf),
        grid_spec=grid_spec,
        out_shape=jax.ShapeDtypeStruct((t, d), BF16),
        compiler_params=_params("arbitrary", "arbitrary"),
        name="moe_experts",
    )(tile_expert, n_tiles, xs, w_gate, w_up, w_down)


def _relayout_kernel(a_ref, b_ref, o_ref, *, bounds, deltas):
    j = pl.program_id(1)
    tn = o_ref.shape[1]
    lo = 0
    for hi, delta in zip(bounds, deltas):
        @pl.when(jnp.logical_and(j >= lo, j < hi))
        def _(delta=delta):
            if delta == 0:
                o_ref[...] = a_ref[...].astype(o_ref.dtype)
            else:
                cat = jnp.concatenate([a_ref[...], b_ref[...]], axis=1)
                o_ref[...] = cat[:, delta:delta + tn].astype(o_ref.dtype)
        lo = hi


def _w_narrow_kernel(a_ref, b_ref, o_ref, *, n1, n2):
    lane = lax.broadcasted_iota(jnp.int32, o_ref.shape, 1)
    o_ref[...] = jnp.where(lane < n1, a_ref[...], jnp.where(lane < n2, b_ref[...], 0.0)).astype(o_ref.dtype)


def _w_narrow(w, *, layer, off1, n1, off2, n2, tk):
    k = w.shape[1]
    return pl.pallas_call(
        functools.partial(_w_narrow_kernel, n1=n1, n2=n2),
        grid=(k // tk,),
        in_specs=[pl.BlockSpec((None, tk, LANES), lambda i: (layer, i, off1 // LANES)),
                  pl.BlockSpec((None, tk, LANES), lambda i: (layer, i, off2 // LANES))],
        out_specs=pl.BlockSpec((tk, LANES), lambda i: (i, 0)),
        out_shape=jax.ShapeDtypeStruct((k, LANES), BF16),
        compiler_params=_params("parallel"),
        name="w_narrow",
    )(w, w)


def _w_relayout(w, *, layer, out_cols, bounds, deltas, tk, tn):
    k, cols = w.shape[1], w.shape[2]
    lane_blocks = tn // LANES
    last = (cols - 1) // LANES
    return pl.pallas_call(
        functools.partial(_relayout_kernel, bounds=bounds, deltas=deltas),
        grid=(k // tk, out_cols // tn),
        in_specs=[pl.BlockSpec((None, tk, tn), lambda i, j: (layer, i, j)),
                  pl.BlockSpec((None, tk, LANES), lambda i, j: (layer, i, jnp.minimum((j + 1) * lane_blocks, last)))],
        out_specs=pl.BlockSpec((tk, tn), lambda i, j: (i, j)),
        out_shape=jax.ShapeDtypeStruct((k, out_cols), BF16),
        compiler_params=_params("parallel", "parallel"),
        name="w_relayout",
    )(w, w)


def _chunk_len(length):
    c = SUBLANES
    while c < min(length, CHUNK):
        c *= 2
    return c


def _pad_time(x, lp):
    pad = lp - x.shape[1]
    if pad == 0:
        return x
    return jnp.pad(x, [(0, 0), (0, pad)] + [(0, 0)] * (x.ndim - 2))


def _mixer_group(hb_, hs_, conv_buf, s_gdn, s_gla, s0_layer, prev_gdn, prev_gla, layer, depth, lw, dims):
    (hg, dg, hl, dk, dv, rank) = dims
    bsz, length, _ = hb_.shape
    gw, kw, vw = hg * dg, hl * dk, hl * dv
    o = 0
    qkv_pre = hb_[..., o:o + 3 * gw]; o += 3 * gw
    z = hb_[..., o:o + gw]; o += gw
    gq = hb_[..., o:o + kw]; o += kw
    gk = hb_[..., o:o + kw]; o += kw
    gv = hb_[..., o:o + vw]; o += vw
    gr = hb_[..., o:o + vw]; o += vw
    beta_in, a_in, glr = hs_[..., :hg], hs_[..., hg:2 * hg], hs_[..., 2 * hg:2 * hg + rank]

    conv_w = lw["gdn_conv_w"]
    width = conv_w.shape[0]
    full = jnp.concatenate([conv_buf, qkv_pre], axis=1)
    acc = full[:, 0:length] * conv_w[0]
    for j in range(1, width):
        acc = acc + full[:, j:j + length] * conv_w[j]
    qkv = _silu(acc)
    new_conv = full[:, length:]

    c = _chunk_len(length)
    lp = -(-length // c) * c
    nc = lp // c
    q = _l2n(qkv[..., :gw].reshape(bsz, length, hg, dg)) * (dg ** -0.5)
    k = _l2n(qkv[..., gw:2 * gw].reshape(bsz, length, hg, dg))
    v = qkv[..., 2 * gw:]
    beta = jax.nn.sigmoid(beta_in)
    log_a = -jnp.exp(lw["gdn_a_log"]) * jax.nn.softplus(a_in + lw["gdn_dt_bias"])
    q = _pad_time(q.reshape(bsz, length, gw), lp)
    k = _pad_time(k.reshape(bsz, length, gw), lp)
    v = _pad_time(v, lp)
    beta = _pad_time(beta, lp)
    log_a = _pad_time(log_a, lp)
    hb = max(hg if c < CHUNK else min(hg, 4), LANES // c)
    bb = 4 if (c < CHUNK and bsz % 4 == 0) else 1
    g = jnp.cumsum(log_a.reshape(bsz, nc, c, hg), axis=2)

    def lane_rows(t):
        return jnp.swapaxes(t, 2, 3).reshape(bsz, nc, hg // hb, hb * c // LANES, LANES)

    o_gdn, s_gdn_new = _gdn_scan(q, k, v, lane_rows(g), lane_rows(beta.reshape(bsz, nc, c, hg)),
                                 s_gdn, prev_gdn, s0_layer=s0_layer, layer=layer, depth=depth, c=c, bb=bb, hb=hb)
    o_gdn = o_gdn[:, :length].reshape(bsz, length, hg, dg)
    o_gdn = _rms(o_gdn, lw["gdn_norm_g"]) * _silu(z.reshape(bsz, length, hg, dg))

    lf = jax.nn.log_sigmoid(jnp.dot(glr, lw["gla_gate_w2"], precision=HI) + lw["gla_gate_b2"]) / GLA_GATE_TAU
    o_gla, s_gla_new = _gla_scan(_pad_time(gq * (dk ** -0.5), lp), _pad_time(gk, lp), _pad_time(gv, lp),
                                 _pad_time(lf, lp), s_gla, prev_gla, s0_layer=s0_layer, layer=layer, depth=depth,
                                 c=c, hb=1 if c == CHUNK else min(hl, 2))
    o_gla = o_gla[:, :length].reshape(bsz, length, hl, dv)
    o_gla = _rms(o_gla, lw["gla_norm_g"]) * _silu(gr.reshape(bsz, length, hl, dv))
    return (o_gdn.reshape(bsz * length, gw).astype(BF16), o_gla.reshape(bsz * length, vw).astype(BF16),
            new_conv, s_gdn_new, s_gla_new)


def _moe(x1, x1b, lw, layer, weights):
    n, d = x1.shape
    rg_w, re_w = lw["router_group_w"], lw["router_expert_w"]
    n_groups, n_exp = rg_w.shape[-1], re_w.shape[-1]
    per_group = n_exp // n_groups
    rw = jnp.concatenate([rg_w, re_w], axis=1)
    rw = jnp.pad(rw, ((0, 0), (0, LANES - rw.shape[1] % LANES)))
    logits = _matmul_f32(x1, rw, tm=_tile(n, 512, SUBLANES), name="router")
    group_logits = logits[:, :n_groups] + lw["router_group_b"]
    group_p = jax.nn.softmax(group_logits, axis=-1)
    g_sel = jnp.argmax(group_logits, axis=-1)
    p_sel = jnp.take_along_axis(group_p, g_sel[:, None], axis=1)
    exp_logits = (logits[:, n_groups:n_groups + n_exp] + lw["router_expert_b"]).reshape(n, n_groups, per_group)
    in_group = jnp.take_along_axis(exp_logits, g_sel[:, None, None], axis=1)[:, 0]
    top_v, top_i = lax.top_k(in_group, 2)
    wts = jax.nn.softmax(top_v, axis=-1) * p_sel
    expert_id = (g_sel[:, None] * per_group + top_i).astype(jnp.int32)

    tm = min(MOE_MAX_ROW_TILE, -(-int(MOE_TILE_SLACK * 2 * n / n_exp) // LANES) * LANES)
    flat_e = expert_id.reshape(-1)
    na = flat_e.shape[0]
    counts = jnp.zeros((n_exp,), jnp.int32).at[flat_e].add(1)
    tiles_per = (counts + tm - 1) // tm
    tile_end = jnp.cumsum(tiles_per)
    row_start = (tile_end - tiles_per) * tm
    order = jnp.argsort(flat_e, stable=True)
    sorted_e = flat_e[order]
    first = jnp.cumsum(counts) - counts
    rank = jnp.arange(na, dtype=jnp.int32) - first[sorted_e]
    dest_sorted = row_start[sorted_e] + rank
    n_rows_tiles = (na + n_exp * (tm - 1)) // tm + 1
    t_rows = n_rows_tiles * tm
    row_token = (jnp.arange(t_rows, dtype=jnp.int32) % n).at[dest_sorted].set(order // 2)
    dest = jnp.zeros((na,), jnp.int32).at[order].set(dest_sorted)
    n_tiles = tile_end[-1:]
    tile_ids = jnp.arange(n_rows_tiles, dtype=jnp.int32)
    tile_expert = jnp.searchsorted(tile_end, jnp.minimum(tile_ids, n_tiles[0] - 1), side="right").astype(jnp.int32)
    tile_expert = jnp.minimum(tile_expert, n_exp - 1)

    xs = x1b[row_token]
    fdim = weights["expert_w_gate"].shape[-1]
    ys = _moe_experts(xs, tile_expert, n_tiles.astype(jnp.int32), weights["expert_w_gate"],
                      weights["expert_w_up"], weights["expert_w_down"], layer=layer, tm=tm,
                      tk=_tile(d, MOE_K_TILE, LANES), tf=_tile(fdim, 256, LANES))
    dest = dest.reshape(n, 2)
    return ys[dest[:, 0]], ys[dest[:, 1]], wts


def kernel(x_prompt, x_sample, state_gdn_conv, state_gdn, state_gla, w_in, gdn_conv_w, gdn_a_log, gdn_dt_bias,
           gdn_norm_g, gla_gate_w2, gla_gate_b2, gla_norm_g, w_proj_gdn, w_proj_gla, w_out, ln_mix_g, ln_mix_b,
           router_group_w, router_group_b, router_expert_w, router_expert_b,
           expert_w_gate, expert_w_up, expert_w_down, ln_ffn_g, ln_ffn_b):
    depth, d, _ = w_in.shape
    bp, lp_, _ = x_prompt.shape
    bs, ls_, _ = x_sample.shape
    hg, dg = gdn_a_log.shape[-1], gdn_norm_g.shape[-1]
    gw = hg * dg
    rank, kw = gla_gate_w2.shape[1], gla_gate_w2.shape[2]
    dv = gla_norm_g.shape[-1]
    vw = w_proj_gla.shape[1]
    hl = vw // dv
    dk = kw // hl
    width = gdn_conv_w.shape[1]
    alpha = (2 * depth) ** 0.25
    dims = (hg, dg, hl, dk, dv, rank)

    np_, ns_ = bp * lp_, bs * ls_
    n = np_ + ns_
    x = jnp.concatenate([x_prompt.reshape(np_, d), x_sample.reshape(ns_, d)], axis=0)
    xb = x.astype(BF16)

    sizes = (3 * gw, hg, hg, gw, kw, kw, vw, rank, vw, d, d)
    offs = [0]
    for sz in sizes:
        offs.append(offs[-1] + sz)
    big_order = (0, 3, 4, 5, 6, 8, 9, 10)
    small_order = (1, 2, 7)
    big_off = {}
    acc = 0
    for i in big_order:
        big_off[i] = acc
        acc += sizes[i]
    big_cols = acc
    ma_off, mb_off = big_off[9], big_off[10]
    small_cols = sum(sizes[i] for i in small_order)
    fused_prompt = (lp_ % CHUNK == 0 and lp_ % GLA_CHUNK == 0 and 2 * hg + rank <= LANES and width - 1 <= SUBLANES
                    and LANES % CHUNK == 0 and hg % (LANES // CHUNK) == 0
                    and big_off[4] % dk == 0 and big_off[5] % dk == 0
                    and big_off[6] % dv == 0 and big_off[8] % dv == 0)

    tm = _tile(n, 1088, SUBLANES)
    tr = _tile(n, 256, SUBLANES)
    tn_in = _tile(big_cols, 512, LANES)
    bounds = (sizes[0] // tn_in, big_off[8] // tn_in, big_cols // tn_in)
    deltas = (0, 2 * hg, 2 * hg + rank)
    pallas_relayout = (sizes[0] % tn_in == 0 and big_off[8] % tn_in == 0 and deltas[-1] < LANES
                       and offs[1] % LANES == 0 and offs[7] % LANES == 2 * hg)
    zeros_conv = jnp.zeros((bp, width - 1, 3 * gw), F32)
    zeros_gdn = jnp.zeros((1, bp, hg, dg, dg), F32)
    zeros_gla = jnp.zeros((1, bp, hl, dk, dv), F32)

    conv_p, conv_s = [], []
    gp = sp = gs = ss = None
    for l in range(depth):
        lw = dict(gdn_conv_w=gdn_conv_w[l], gdn_a_log=gdn_a_log[l], gdn_dt_bias=gdn_dt_bias[l],
                  gdn_norm_g=gdn_norm_g[l], gla_gate_w2=gla_gate_w2[l], gla_gate_b2=gla_gate_b2[l],
                  gla_norm_g=gla_norm_g[l], router_group_w=router_group_w[l], router_group_b=router_group_b[l],
                  router_expert_w=router_expert_w[l], router_expert_b=router_expert_b[l])
        if pallas_relayout:
            w_big = _w_relayout(w_in, layer=l, out_cols=big_cols, bounds=bounds, deltas=deltas,
                                tk=_tile(d, 1024, SUBLANES), tn=tn_in)
            w_small = _w_narrow(w_in, layer=l, off1=offs[1], n1=2 * hg, off2=offs[7], n2=2 * hg + rank,
                                tk=_tile(d, 1024, SUBLANES))
        else:
            wl = w_in[l]
            w_big = jnp.concatenate([wl[:, offs[i]:offs[i + 1]] for i in big_order], axis=1).astype(BF16)
            w_small = jnp.concatenate([wl[:, offs[i]:offs[i + 1]] for i in small_order], axis=1)
            w_small = jnp.pad(w_small, ((0, 0), (0, LANES - small_cols))).astype(BF16)

        h_big = _matmul(xb, w_big, tm=tm, tn=tn_in, name="in_proj")
        h_small = _matmul(xb, w_small, tm=tm, tn=LANES, name="in_proj_small")

        if fused_prompt:
            og_p, gp, cq, ck, cv = _gdn_prompt(h_big, h_small, lw["gdn_conv_w"], lw["gdn_a_log"],
                                               lw["gdn_dt_bias"], lw["gdn_norm_g"], gp, layer=l, depth=depth,
                                               bsz=bp, length=lp_, hg=hg, dg=dg)
            cp = jnp.concatenate([t[:, SUBLANES - (width - 1):] for t in (cq, ck, cv)], axis=-1)
            w2_rows = jnp.pad(lw["gla_gate_w2"], ((2 * hg, LANES - 2 * hg - rank), (0, 0)))
            ol_p, sp = _gla_prompt(h_big, h_small, w2_rows, lw["gla_gate_b2"], lw["gla_norm_g"],
                                   (big_off[4], big_off[5], big_off[6], big_off[8]), sp, layer=l, depth=depth,
                                   bsz=bp, length=lp_, hl=hl, dk=dk, dv=dv)
        else:
            og_p, ol_p, cp, gp, sp = _mixer_group(h_big[:np_].reshape(bp, lp_, big_cols),
                                                  h_small[:np_].reshape(bp, lp_, LANES),
                                                  zeros_conv, zeros_gdn, zeros_gla, 0, gp, sp, l, depth, lw, dims)
        og_s, ol_s, cs, gs, ss = _mixer_group(h_big[np_:].reshape(bs, ls_, big_cols),
                                              h_small[np_:].reshape(bs, ls_, LANES),
                                              state_gdn_conv[l], state_gdn, state_gla, l, gs, ss, l, depth, lw, dims)
        og = jnp.concatenate([og_p, og_s], axis=0)
        ol = jnp.concatenate([ol_p, ol_s], axis=0)
        merged = _merge_proj(og, ol, w_proj_gdn, w_proj_gla, h_big, ma_off, mb_off, layer=l, tm=tm,
                             tn=_tile(d, 256, LANES))
        mix = _matmul(merged, w_out, layer=l, tm=tm, tn=_tile(d, 512, LANES), name="out_proj")
        x1, x1b = _deepnorm_ln(x, mix, ln_mix_g[l], ln_mix_b[l], alpha=alpha, tr=tr)
        ya, yb, wts = _moe(x1, x1b, lw, l, dict(expert_w_gate=expert_w_gate, expert_w_up=expert_w_up,
                                               expert_w_down=expert_w_down))
        x, xb = _combine_ln(x1, ya, yb, wts, ln_ffn_g[l], ln_ffn_b[l], alpha=alpha, tr=tr)
        conv_p.append(cp)
        conv_s.append(cs)

    return (x[:np_].reshape(bp, lp_, d), x[np_:].reshape(bs, ls_, d),
            jnp.stack(conv_p), gp, sp, jnp.stack(conv_s), gs, ss)
```
